```python
import math, functools
import jax, jax.numpy as jnp
from jax import lax
import numpy as np

D_MODEL = 1024
BATCH = 4
SEQ = 4096
DEPTH = 2
DEC_BATCH = 128
DEC_SEQ = 8
PAST_LEN = 8192
PAGE_SIZE = 128

N_META = 16
Q_BLOCK = 128
EPS = 1e-6
NEG_INF = -1e30
N_MIXERS = 2
N_A_LAYERS = (DEPTH + 1) // 2
N_B_LAYERS = DEPTH // 2
A_HEADS = 8
A_HEAD_DIM = 64
A_VDIM = 2 * A_HEAD_DIM
A_QW = A_HEADS * 2 * A_HEAD_DIM
N_BUCKETS = 32
MAX_DISTANCE = 128
B_HEADS = 8
B_NOPE = 128
B_ROPE = 64
B_VDIM = 128
Q_LORA = 256
KV_LORA = 256
ROPE_THETA = 10000.0
MLA_SCALE = (B_NOPE + B_ROPE) ** -0.5
D_FF = 2816
CONV_W = 3

kernel_name = "hybrid_diffattn_mla_convffn_step"


def _rmsnorm(x, g):
    xf = x.astype(jnp.float32)
    y = xf * lax.rsqrt(jnp.mean(xf * xf, axis=-1, keepdims=True) + EPS)
    return (y * g.astype(jnp.float32)).astype(x.dtype)


def _rope(x, pos):
    half = x.shape[-1] // 2
    inv = jnp.power(ROPE_THETA, -jnp.arange(half, dtype=jnp.float32) / half)
    ang = pos.astype(jnp.float32)[..., None] * inv
    cos, sin = jnp.cos(ang), jnp.sin(ang)
    xf = x.astype(jnp.float32)
    x1, x2 = xf[..., :half], xf[..., half:]
    return jnp.concatenate([x1 * cos - x2 * sin, x1 * sin + x2 * cos], axis=-1).astype(x.dtype)


def _t5_bucket(rel):
    n = jnp.maximum(-rel, 0)
    max_exact = N_BUCKETS // 2
    nf = jnp.maximum(n, 1).astype(jnp.float32)
    log_b = max_exact + (jnp.log(nf / max_exact) / math.log(MAX_DISTANCE / max_exact)
                         * (N_BUCKETS - max_exact)).astype(jnp.int32)
    return jnp.where(n < max_exact, n, jnp.minimum(log_b, N_BUCKETS - 1))


def _diff_qkv(h, w_qkv):
    u = h @ w_qkv
    lead = h.shape[:-1]
    q = u[..., :A_QW].reshape(lead + (A_HEADS, 2, A_HEAD_DIM))
    k = u[..., A_QW:2 * A_QW].reshape(lead + (A_HEADS, 2, A_HEAD_DIM))
    v = u[..., 2 * A_QW:].reshape(lead + (A_HEADS, A_VDIM))
    return q, k, v


def _diff_attend(qs, qpos, ks, kpos, rel_table, lam):
    (q,) = qs
    k, v = ks
    logits = jnp.einsum('qhcd,khcd->hcqk', q, k).astype(jnp.float32) * (A_HEAD_DIM ** -0.5)
    bias = rel_table[_t5_bucket(kpos[None, :] - qpos[:, None])].astype(jnp.float32)
    logits = logits + jnp.transpose(bias, (2, 0, 1))[:, None]
    logits = jnp.where(kpos[None, :] <= qpos[:, None], logits, NEG_INF)
    p = jax.nn.softmax(logits, axis=-1)
    a = p[:, 0] - lam * p[:, 1]
    return jnp.einsum('hqk,khe->qhe', a.astype(v.dtype), v)


def _diff_out(o, lam_init, subln_g, w_o):
    o = _rmsnorm(o, subln_g) * (1.0 - lam_init)
    return o.reshape(o.shape[:-2] + (A_HEADS * A_VDIM,)) @ w_o


def _mla_proj(h, pos, w_dqkv, q_norm_g, w_uq, kv_norm_g, w_uk):
    a = h @ w_dqkv
    cq = a[..., :Q_LORA]
    ckv = a[..., Q_LORA:Q_LORA + KV_LORA]
    kpe = a[..., Q_LORA + KV_LORA:]
    q = (_rmsnorm(cq, q_norm_g) @ w_uq).reshape(h.shape[:-1] + (B_HEADS, B_NOPE + B_ROPE))
    q_nope, q_pe = q[..., :B_NOPE], q[..., B_NOPE:]
    q_lat = jnp.einsum('...hn,chn->...hc', q_nope, w_uk)
    q_pe = _rope(q_pe, pos[:, None])
    c_kv = _rmsnorm(ckv, kv_norm_g)
    k_pe = _rope(kpe, pos)
    return q_lat, q_pe, c_kv, k_pe


def _mla_attend(qs, qpos, ks, kpos):
    q_lat, q_pe = qs
    c_kv, k_pe = ks
    logits = (jnp.einsum('qhc,kc->hqk', q_lat, c_kv)
              + jnp.einsum('qhr,kr->hqk', q_pe, k_pe)).astype(jnp.float32) * MLA_SCALE
    logits = jnp.where(kpos[None, :] <= qpos[:, None], logits, NEG_INF)
    p = jax.nn.softmax(logits, axis=-1)
    return jnp.einsum('hqk,kc->qhc', p.astype(c_kv.dtype), c_kv)


def _mla_out(o_lat, w_uv, w_o):
    o = jnp.einsum('...hc,chv->...hv', o_lat, w_uv)
    return o.reshape(o.shape[:-2] + (B_HEADS * B_VDIM,)) @ w_o


def _prompt_attention(attend, qs, ks, pos):
    def one_seq(qs1, ks1):
        meta = attend([q[:N_META] for q in qs1], pos[:N_META], [k[:N_META] for k in ks1], pos[:N_META])
        nb = (pos.shape[0] - N_META) // Q_BLOCK
        qb = [q[N_META:].reshape((nb, Q_BLOCK) + q.shape[1:]) for q in qs1]
        pb = pos[N_META:].reshape(nb, Q_BLOCK)
        blk = lax.map(lambda args: attend(args[0], args[1], ks1, pos), (qb, pb))
        blk = blk.reshape((nb * Q_BLOCK,) + blk.shape[2:])
        return jnp.concatenate([meta, blk], axis=0)
    return jax.vmap(one_seq)(qs, ks)


def _sample_attention(attend, qs, ks_new, pools, page_table, qpos):
    kpos = jnp.arange(PAST_LEN + qpos.shape[0], dtype=jnp.int32)
    def one_seq(args):
        qs1, kn1, pages = args
        ks1 = [jnp.concatenate([pool[pages].reshape((-1,) + pool.shape[2:]), kn], axis=0)
               for pool, kn in zip(pools, kn1)]
        return attend(qs1, qpos, ks1, kpos)
    return lax.map(one_seq, (qs, ks_new, page_table))


def _conv_ffn(h, conv_state, w_in, conv_w, conv_b, w_out):
    u = h @ w_in
    gate, val = u[..., :D_FF], u[..., D_FF:]
    ext = jnp.concatenate([conv_state.astype(gate.dtype), gate], axis=1)
    L = h.shape[1]
    conv = conv_b + sum(ext[:, j:j + L] * conv_w[j] for j in range(CONV_W))
    y = (jax.nn.gelu(conv, approximate=True) * val) @ w_out
    return y, ext[:, L:]


def setup_inputs(seed: int = 0) -> dict:
    key = jax.random.key(seed)
    ks = jax.random.split(key, 32)
    n_pages = PAST_LEN // PAGE_SIZE
    n_used = DEC_BATCH * n_pages
    n_pool = n_used + n_used // 4

    def nrm(k, shape, s):
        return jax.random.normal(k, shape, jnp.float32) * s

    page_table = jax.random.permutation(ks[0], n_pool)[:n_used].reshape(DEC_BATCH, n_pages).astype(jnp.int32)
    return {
        "x_prompt": nrm(ks[1], (BATCH, SEQ, D_MODEL), 1.0),
        "x_sample": nrm(ks[2], (DEC_BATCH, DEC_SEQ, D_MODEL), 1.0),
        "cache_k_a": nrm(ks[3], (N_A_LAYERS, n_pool, PAGE_SIZE, A_HEADS, 2, A_HEAD_DIM), 1.0),
        "cache_v_a": nrm(ks[4], (N_A_LAYERS, n_pool, PAGE_SIZE, A_HEADS, A_VDIM), 1.0),
        "cache_ckv_b": nrm(ks[5], (N_B_LAYERS, n_pool, PAGE_SIZE, KV_LORA), 1.0),
        "cache_kpe_b": nrm(ks[6], (N_B_LAYERS, n_pool, PAGE_SIZE, B_ROPE), 1.0),
        "state_conv": nrm(ks[7], (DEPTH, DEC_BATCH, CONV_W - 1, D_FF), 1.0),
        "page_table": page_table,
        "meta_tokens": nrm(ks[8], (N_META, D_MODEL), 1.0),
        "rel_bias": nrm(ks[9], (N_BUCKETS, A_HEADS), 0.5),
        "norm_g": 1.0 + nrm(ks[10], (DEPTH, 4, D_MODEL), 0.05),
        "w_qkv_a": nrm(ks[11], (N_A_LAYERS, D_MODEL, 3 * A_QW), D_MODEL ** -0.5),
        "lambda_q1": nrm(ks[12], (N_A_LAYERS, A_HEAD_DIM), 0.1),
        "lambda_k1": nrm(ks[13], (N_A_LAYERS, A_HEAD_DIM), 0.1),
        "lambda_q2": nrm(ks[14], (N_A_LAYERS, A_HEAD_DIM), 0.1),
        "lambda_k2": nrm(ks[15], (N_A_LAYERS, A_HEAD_DIM), 0.1),
        "subln_g": 1.0 + nrm(ks[16], (N_A_LAYERS, A_VDIM), 0.05),
        "w_o_a": nrm(ks[17], (N_A_LAYERS, A_HEADS * A_VDIM, D_MODEL), (A_HEADS * A_VDIM) ** -0.5),
        "w_dqkv_b": nrm(ks[18], (N_B_LAYERS, D_MODEL, Q_LORA + KV_LORA + B_ROPE), D_MODEL ** -0.5),
        "q_norm_g": 1.0 + nrm(ks[19], (N_B_LAYERS, Q_LORA), 0.05),
        "w_uq": nrm(ks[20], (N_B_LAYERS, Q_LORA, B_HEADS * (B_NOPE + B_ROPE)), Q_LORA ** -0.5),
        "kv_norm_g": 1.0 + nrm(ks[21], (N_B_LAYERS, KV_LORA), 0.05),
        "w_uk": nrm(ks[22], (N_B_LAYERS, KV_LORA, B_HEADS, B_NOPE), KV_LORA ** -0.5),
        "w_uv": nrm(ks[23], (N_B_LAYERS, KV_LORA, B_HEADS, B_VDIM), KV_LORA ** -0.5),
        "w_o_b": nrm(ks[24], (N_B_LAYERS, B_HEADS * B_VDIM, D_MODEL), (B_HEADS * B_VDIM) ** -0.5),
        "w_ffn_in": nrm(ks[25], (DEPTH, D_MODEL, 2 * D_FF), D_MODEL ** -0.5),
        "conv_w": nrm(ks[26], (DEPTH, CONV_W, D_FF), CONV_W ** -0.5),
        "conv_b": nrm(ks[27], (DEPTH, D_FF), 0.02),
        "w_ffn_out": nrm(ks[28], (DEPTH, D_FF, D_MODEL), D_FF ** -0.5),
    }


def reference(x_prompt, x_sample, cache_k_a, cache_v_a, cache_ckv_b, cache_kpe_b, state_conv, page_table,
              meta_tokens, rel_bias, norm_g, w_qkv_a, lambda_q1, lambda_k1, lambda_q2, lambda_k2, subln_g, w_o_a,
              w_dqkv_b, q_norm_g, w_uq, kv_norm_g, w_uk, w_uv, w_o_b, w_ffn_in, conv_w, conv_b, w_ffn_out):
    b_p, seq = x_prompt.shape[0], x_prompt.shape[1]
    meta = jnp.broadcast_to(meta_tokens[None], (b_p, N_META, D_MODEL)).astype(x_prompt.dtype)
    xp = jnp.concatenate([meta, x_prompt], axis=1)
    xs = x_sample
    pos_p = jnp.arange(N_META + seq, dtype=jnp.int32)
    pos_s = PAST_LEN + jnp.arange(x_sample.shape[1], dtype=jnp.int32)
    conv0_p = jnp.zeros((b_p, CONV_W - 1, D_FF), x_prompt.dtype)

    k_a_p, v_a_p, k_a_s, v_a_s = [], [], [], []
    ckv_p, kpe_p, ckv_s, kpe_s = [], [], [], []
    conv_p, conv_s = [], []
    for i in range(DEPTH):
        g = norm_g[i]
        hp, hs = _rmsnorm(xp, g[0]), _rmsnorm(xs, g[0])
        j = i // N_MIXERS
        if i % N_MIXERS == 0:
            lam_init = 0.8 - 0.6 * math.exp(-0.3 * i)
            lam = (jnp.exp(jnp.sum(lambda_q1[j].astype(jnp.float32) * lambda_k1[j].astype(jnp.float32)))
                   - jnp.exp(jnp.sum(lambda_q2[j].astype(jnp.float32) * lambda_k2[j].astype(jnp.float32)))
                   + lam_init)
            attend = functools.partial(_diff_attend, rel_table=rel_bias, lam=lam)
            qp, kp, vp = _diff_qkv(hp, w_qkv_a[j])
            qs_, ks_, vs_ = _diff_qkv(hs, w_qkv_a[j])
            op = _prompt_attention(attend, (qp,), (kp, vp), pos_p)
            os_ = _sample_attention(attend, (qs_,), (ks_, vs_), (cache_k_a[j], cache_v_a[j]), page_table, pos_s)
            mp = _diff_out(op, lam_init, subln_g[j], w_o_a[j])
            ms = _diff_out(os_, lam_init, subln_g[j], w_o_a[j])
            k_a_p.append(kp); v_a_p.append(vp); k_a_s.append(ks_); v_a_s.append(vs_)
        else:
            qlp, qpp, ckp, kpp = _mla_proj(hp, pos_p, w_dqkv_b[j], q_norm_g[j], w_uq[j], kv_norm_g[j], w_uk[j])
            qls, qps, cks, kps = _mla_proj(hs, pos_s, w_dqkv_b[j], q_norm_g[j], w_uq[j], kv_norm_g[j], w_uk[j])
            op = _prompt_attention(_mla_attend, (qlp, qpp), (ckp, kpp), pos_p)
            os_ = _sample_attention(_mla_attend, (qls, qps), (cks, kps), (cache_ckv_b[j], cache_kpe_b[j]),
                                    page_table, pos_s)
            mp = _mla_out(op, w_uv[j], w_o_b[j])
            ms = _mla_out(os_, w_uv[j], w_o_b[j])
            ckv_p.append(ckp); kpe_p.append(kpp); ckv_s.append(cks); kpe_s.append(kps)
        xp = xp + _rmsnorm(mp, g[1])
        xs = xs + _rmsnorm(ms, g[1])
        fp, cp = _conv_ffn(_rmsnorm(xp, g[2]), conv0_p, w_ffn_in[i], conv_w[i], conv_b[i], w_ffn_out[i])
        fs, cs = _conv_ffn(_rmsnorm(xs, g[2]), state_conv[i], w_ffn_in[i], conv_w[i], conv_b[i], w_ffn_out[i])
        xp = xp + _rmsnorm(fp, g[3])
        xs = xs + _rmsnorm(fs, g[3])
        conv_p.append(cp); conv_s.append(cs)

    return (xp[:, N_META:], xs,
            jnp.stack(k_a_p), jnp.stack(v_a_p), jnp.stack(k_a_s), jnp.stack(v_a_s),
            jnp.stack(ckv_p), jnp.stack(kpe_p), jnp.stack(ckv_s), jnp.stack(kpe_s),
            jnp.stack(conv_p), jnp.stack(conv_s))
```

```python
import functools
import math

import jax
import jax.numpy as jnp
from jax import lax
from jax.experimental import pallas as pl
from jax.experimental.pallas import tpu as pltpu

F32 = jnp.float32
BF16 = jnp.bfloat16

EPS = 1e-6
NEG_INF = -1e30
LOG2E = 1.4426950408889634
N_BUCKETS = 32
MAX_DISTANCE = 128
ROPE_THETA = 10000.0
LANES = 128
ATTN_TILE = 256
VMEM_LIMIT = 56 * 1024 * 1024


def _cparams(sem):
    return pltpu.CompilerParams(dimension_semantics=sem, vmem_limit_bytes=VMEM_LIMIT)


def _const_spec(shape):
    nd = len(shape)
    return pl.BlockSpec(shape, lambda *_: (0,) * nd, pipeline_mode=pl.Buffered(1))


def _rms(x, g):
    return x * lax.rsqrt(jnp.mean(x * x, axis=-1, keepdims=True) + EPS) * g


def _dot(a, b):
    return jnp.dot(a, b, preferred_element_type=F32)


def _dot_nt(a, b):
    return lax.dot_general(a, b, (((1,), (1,)), ((), ())), preferred_element_type=F32)


def _row_tile(m, cap):
    t = min(m, cap)
    while m % t:
        t //= 2
    return t


def _bias_kernel(tbl_ref, o_ref, *, segs):
    h = pl.program_id(0)
    rows = o_ref.shape[0]
    col0 = 0
    max_exact = N_BUCKETS // 2
    for c_off, width in segs:
        r = lax.broadcasted_iota(jnp.int32, (rows, width), 0)
        c = lax.broadcasted_iota(jnp.int32, (rows, width), 1)
        n = jnp.maximum(r - c - c_off, 0)
        nf = jnp.maximum(n, 1).astype(F32)
        log_b = max_exact + (jnp.log(nf / max_exact) / math.log(MAX_DISTANCE / max_exact)
                             * (N_BUCKETS - max_exact)).astype(jnp.int32)
        bucket = jnp.where(n < max_exact, n, jnp.minimum(log_b, N_BUCKETS - 1))
        val = jnp.full((rows, width), tbl_ref[N_BUCKETS - 1, h], F32)
        for b in range(N_BUCKETS - 1):
            val = jnp.where(bucket == b, tbl_ref[b, h], val)
        o_ref[:, col0:col0 + width] = val * LOG2E
        col0 += width


def _bias_tiles(rel_bias, rows, segs):
    n_heads = rel_bias.shape[1]
    width = sum(w for _, w in segs)
    return pl.pallas_call(
        functools.partial(_bias_kernel, segs=segs),
        out_shape=jax.ShapeDtypeStruct((n_heads, rows, width), F32),
        grid=(n_heads,),
        in_specs=[pl.BlockSpec(memory_space=pltpu.SMEM)],
        out_specs=pl.BlockSpec((None, rows, width), lambda h: (h, 0, 0)),
        compiler_params=_cparams(("arbitrary",)),
        name="rel_bias_tiles",
    )(rel_bias)


def _qkv_kernel(x_ref, g_ref, w_ref, q_ref, k_ref, v_ref, kb_ref, vb_ref, *, qscale):
    n = q_ref.shape[1]
    xn = _rms(x_ref[...], g_ref[...]).astype(BF16)
    q_ref[...] = (_dot(xn, w_ref[:, :n]) * qscale).astype(BF16)
    k = _dot(xn, w_ref[:, n:2 * n])
    k_ref[...] = k
    kb_ref[...] = k.astype(BF16)
    v = _dot(xn, w_ref[:, 2 * n:])
    v_ref[...] = v
    vb_ref[...] = v.astype(BF16)


def _qkv_proj(x, g, w, qscale):
    m, d = x.shape
    n = w.shape[1] // 3
    tm = _row_tile(m, 512)
    row = lambda width: pl.BlockSpec((tm, width), lambda i: (i, 0))
    return pl.pallas_call(
        functools.partial(_qkv_kernel, qscale=qscale),
        out_shape=(jax.ShapeDtypeStruct((m, n), BF16), jax.ShapeDtypeStruct((m, n), F32),
                   jax.ShapeDtypeStruct((m, n), F32), jax.ShapeDtypeStruct((m, n), BF16),
                   jax.ShapeDtypeStruct((m, n), BF16)),
        grid=(m // tm,),
        in_specs=[row(d), _const_spec((1, d)), _const_spec(w.shape)],
        out_specs=(row(n), row(n), row(n), row(n), row(n)),
        compiler_params=_cparams(("arbitrary",)),
        name="qkv_proj",
    )(x, g, w)


def _flash_kernel(*refs, diff, has_main, tq, n_meta, lam_init, bias_meta_off):
    if diff:
        (b31_ref, lam_ref, sg_ref, q_ref, k_ref, v_ref, km_ref, vm_ref, bias_ref,
         o_ref, m_ref, l_ref, acc_ref, q2_ref) = refs
    else:
        q_ref, k_ref, v_ref, km_ref, vm_ref, o_ref, m_ref, l_ref, acc_ref = refs
    h = pl.program_id(1)
    qi = pl.program_id(2)
    T = ATTN_TILE

    if diff:
        q = q_ref[...]
        lane = lax.broadcasted_iota(jnp.int32, q.shape, 1)
        half = q.shape[1] // 2
        q2_ref[:tq] = jnp.where(lane < half, q, jnp.zeros_like(q))
        q2_ref[tq:] = jnp.where(lane >= half, q, jnp.zeros_like(q))
        qq = q2_ref[...]
        b31 = b31_ref[h]
    else:
        qq = q_ref[...]
        b31 = None
    stack = (lambda a: jnp.concatenate([a, a], axis=0)) if diff else (lambda a: a)

    m_ref[...] = jnp.full(m_ref.shape, NEG_INF, F32)
    l_ref[...] = jnp.zeros(l_ref.shape, F32)
    acc_ref[...] = jnp.zeros(acc_ref.shape, F32)

    def update(kt, vt, bias, mask):
        s = _dot_nt(qq, kt)
        if bias is not None:
            s = s + bias
        if mask is not None:
            s = jnp.where(mask, s, NEG_INF)
        m_prev = m_ref[...]
        m_new = jnp.maximum(m_prev, jnp.max(s, axis=-1, keepdims=True))
        alpha = jnp.exp2(m_prev - m_new)
        p = jnp.exp2(s - m_new)
        l_ref[...] = alpha * l_ref[...] + jnp.sum(p, axis=-1, keepdims=True)
        acc_ref[...] = alpha * acc_ref[...] + _dot(p.astype(BF16), vt)
        m_ref[...] = m_new

    r = lax.broadcasted_iota(jnp.int32, (tq, LANES), 0)
    c = lax.broadcasted_iota(jnp.int32, (tq, LANES), 1)
    if has_main:
        meta_mask = c < n_meta
        if diff:
            near = bias_ref[:, bias_meta_off:bias_meta_off + LANES]
            meta_bias = stack(jnp.where(qi == 0, near, b31))
        else:
            meta_bias = None
    else:
        meta_mask = (c < n_meta) & (c <= r)
        meta_bias = stack(bias_ref[...]) if diff else None
    update(km_ref[...], vm_ref[...], meta_bias, stack(meta_mask))

    if has_main:
        def far_body(kt_i, carry):
            off = pl.multiple_of(kt_i * T, T)
            update(k_ref[pl.ds(off, T), :], v_ref[pl.ds(off, T), :], b31, None)
            return carry
        lax.fori_loop(0, jnp.maximum(qi - 1, 0), far_body, 0)

        @pl.when(qi >= 1)
        def _prev():
            off = pl.multiple_of((qi - 1) * T, T)
            bias = stack(bias_ref[:, 0:T]) if diff else None
            update(k_ref[pl.ds(off, T), :], v_ref[pl.ds(off, T), :], bias, None)

        off = pl.multiple_of(qi * T, T)
        rr = lax.broadcasted_iota(jnp.int32, (tq, T), 0)
        cc = lax.broadcasted_iota(jnp.int32, (tq, T), 1)
        bias = stack(bias_ref[:, T:2 * T]) if diff else None
        update(k_ref[pl.ds(off, T), :], v_ref[pl.ds(off, T), :], bias, stack(cc <= rr))

    a = acc_ref[...] / l_ref[...]
    if diff:
        lv = lam_ref[...]
        lam = (jnp.exp(jnp.sum(lv[0:1] * lv[1:2], axis=-1, keepdims=True))
               - jnp.exp(jnp.sum(lv[2:3] * lv[3:4], axis=-1, keepdims=True)) + lam_init)
        o = a[:tq] - lam * a[tq:]
        o = _rms(o, sg_ref[...]) * (1.0 - lam_init)
    else:
        o = a
    o_ref[...] = o.astype(o_ref.dtype)


def _flash_diff(q, kb, vb, km, vm, bias, b31, lam_vecs, subln_g, *, n_seq, n_heads, has_main, n_meta,
                lam_init, bias_meta_off):
    m, hd = q.shape
    dv = hd // n_heads
    seq_len = m // n_seq
    tq = ATTN_TILE if has_main else seq_len
    nq = seq_len // tq
    kern = functools.partial(_flash_kernel, diff=True, has_main=has_main, tq=tq, n_meta=n_meta,
                             lam_init=lam_init, bias_meta_off=bias_meta_off)
    qspec = pl.BlockSpec((tq, dv), lambda b, h, i: (b * nq + i, h))
    kvspec = pl.BlockSpec((seq_len, dv), lambda b, h, i: (b, h))
    mspec = pl.BlockSpec((LANES, dv), lambda b, h, i: (0, h))
    return pl.pallas_call(
        kern,
        out_shape=jax.ShapeDtypeStruct((m, hd), BF16),
        grid=(n_seq, n_heads, nq),
        in_specs=[pl.BlockSpec(memory_space=pltpu.SMEM), _const_spec(lam_vecs.shape),
                  _const_spec(subln_g.shape), qspec, kvspec, kvspec, mspec, mspec,
                  pl.BlockSpec((None,) + bias.shape[1:], lambda b, h, i: (h, 0, 0))],
        out_specs=qspec,
        scratch_shapes=[pltpu.VMEM((2 * tq, 1), F32), pltpu.VMEM((2 * tq, 1), F32),
                        pltpu.VMEM((2 * tq, dv), F32), pltpu.VMEM((2 * tq, dv), BF16)],
        compiler_params=_cparams(("arbitrary", "arbitrary", "arbitrary")),
        name="diff_flash" if has_main else "diff_meta_attn",
    )(b31, lam_vecs, subln_g, q, kb, vb, km, vm, bias)


def _flash_mla(q, kc, vb, km, vm, *, n_seq, has_main, n_meta):
    n_heads, m, dq = q.shape
    dv = vb.shape[1] // n_heads
    seq_len = m // n_seq
    tq = ATTN_TILE if has_main else seq_len
    nq = seq_len // tq
    kern = functools.partial(_flash_kernel, diff=False, has_main=has_main, tq=tq, n_meta=n_meta,
                             lam_init=0.0, bias_meta_off=0)
    return pl.pallas_call(
        kern,
        out_shape=jax.ShapeDtypeStruct((m, n_heads * dv), BF16),
        grid=(n_seq, n_heads, nq),
        in_specs=[pl.BlockSpec((None, tq, dq), lambda b, h, i: (h, b * nq + i, 0)),
                  pl.BlockSpec((None, seq_len, dq), lambda b, h, i: (h, b, 0)),
                  pl.BlockSpec((seq_len, dv), lambda b, h, i: (b, h)),
                  pl.BlockSpec((None, LANES, dq), lambda b, h, i: (h, 0, 0)),
                  pl.BlockSpec((LANES, dv), lambda b, h, i: (0, h))],
        out_specs=pl.BlockSpec((tq, dv), lambda b, h, i: (b * nq + i, h)),
        scratch_shapes=[pltpu.VMEM((tq, 1), F32), pltpu.VMEM((tq, 1), F32), pltpu.VMEM((tq, dv), F32)],
        compiler_params=_cparams(("arbitrary", "arbitrary", "arbitrary")),
        name="mla_flash" if has_main else "mla_meta_attn",
    )(q, kc, vb, km, vm)


def _online_update(s, m_ref, l_ref):
    m_prev = m_ref[...]
    m_new = jnp.maximum(m_prev, jnp.max(s, axis=-1, keepdims=True))
    alpha = jnp.exp2(m_prev - m_new)
    p = jnp.exp2(s - m_new)
    l_ref[...] = alpha * l_ref[...] + jnp.sum(p, axis=-1, keepdims=True)
    m_ref[...] = m_new
    return alpha, p.astype(BF16)


def _diff_decode_kernel(pt_ref, *refs, n_pg, page, n_pair, lam_init):
    del pt_ref
    k_pages = refs[:n_pg]
    v_pages = refs[n_pg:2 * n_pg]
    (q_ref, kn_ref, vn_ref, bfar_ref, blast_ref, bnew_ref, lam_ref, sg_ref,
     o_ref, kc_ref, vc_ref, m_ref, l_ref, acc_ref) = refs[2 * n_pg:]
    j = pl.program_id(1)
    last = pl.num_programs(1) - 1
    rp = q_ref.shape[1]
    wp = q_ref.shape[2]
    n_t = kn_ref.shape[0]

    @pl.when(j == 0)
    def _init():
        m_ref[...] = jnp.full(m_ref.shape, NEG_INF, F32)
        l_ref[...] = jnp.zeros(l_ref.shape, F32)
        acc_ref[...] = jnp.zeros(acc_ref.shape, F32)

    def attend(n_keys, bias):
        s = jnp.concatenate(
            [_dot_nt(q_ref[pr], kc_ref[0:n_keys, pr * wp:(pr + 1) * wp]) for pr in range(n_pair)], axis=0)
        alpha, p = _online_update(s + bias, m_ref, l_ref)
        for pr in range(n_pair):
            rows = slice(pr * rp, (pr + 1) * rp)
            acc_ref[pr] = alpha[rows] * acc_ref[pr] + _dot(p[rows], vc_ref[0:n_keys, pr * wp:(pr + 1) * wp])

    for i in range(n_pg):
        kc_ref[i * page:(i + 1) * page, :] = k_pages[i][...].astype(BF16)
        vc_ref[i * page:(i + 1) * page, :] = v_pages[i][...].astype(BF16)
    attend(n_pg * page, jnp.where(j == last, blast_ref[...], bfar_ref[...]))

    @pl.when(j == last)
    def _finish():
        kc_ref[0:LANES, :] = jnp.zeros((LANES, kc_ref.shape[1]), BF16)
        vc_ref[0:LANES, :] = jnp.zeros((LANES, vc_ref.shape[1]), BF16)
        kc_ref[0:n_t, :] = kn_ref[...].astype(BF16)
        vc_ref[0:n_t, :] = vn_ref[...].astype(BF16)
        attend(LANES, bnew_ref[...])
        lv = lam_ref[...]
        lam = (jnp.exp(jnp.sum(lv[0:1] * lv[1:2], axis=-1, keepdims=True))
               - jnp.exp(jnp.sum(lv[2:3] * lv[3:4], axis=-1, keepdims=True)) + lam_init)
        inv_l = 1.0 / l_ref[...]
        dv = wp // 2
        for pr in range(n_pair):
            a = acc_ref[pr] * inv_l[pr * rp:(pr + 1) * rp]
            for hl in range(2):
                r0 = hl * 2 * n_t
                o = (a[r0:r0 + n_t, hl * dv:(hl + 1) * dv]
                     - lam * a[r0 + n_t:r0 + 2 * n_t, hl * dv:(hl + 1) * dv])
                o = _rms(o, sg_ref[...]) * (1.0 - lam_init)
                hcol = (2 * pr + hl) * dv
                o_ref[:, hcol:hcol + dv] = o


def _diff_decode(page_table, pool_k, pool_v, qbd, k_new, v_new, b_far, b_last, b_new, lam_vecs, subln_g,
                 *, n_pg, lam_init):
    n_seq, n_pages = page_table.shape
    _, page, width = pool_k.shape
    n_pair, rp, wp = qbd.shape[1:]
    n_t = k_new.shape[1]
    n_chunk = n_pages // n_pg
    n_rows = n_pair * rp

    def page_spec(i):
        return pl.BlockSpec((None, page, width), lambda s, j, pt: (pt[s, j * n_pg + i], 0, 0))

    seq_spec = lambda shape: pl.BlockSpec((None,) + shape, lambda s, j, pt: (s,) + (0,) * len(shape))
    const = lambda shape: pl.BlockSpec(shape, lambda s, j, pt: (0,) * len(shape))
    kern = functools.partial(_diff_decode_kernel, n_pg=n_pg, page=page, n_pair=n_pair, lam_init=lam_init)
    grid_spec = pltpu.PrefetchScalarGridSpec(
        num_scalar_prefetch=1,
        grid=(n_seq, n_chunk),
        in_specs=([page_spec(i) for i in range(n_pg)] + [page_spec(i) for i in range(n_pg)]
                  + [seq_spec((n_pair, rp, wp)), seq_spec((n_t, width)), seq_spec((n_t, width)),
                     const(b_far.shape), const(b_last.shape), const(b_new.shape),
                     const(lam_vecs.shape), const(subln_g.shape)]),
        out_specs=seq_spec((n_t, width)),
        scratch_shapes=[pltpu.VMEM((n_pg * page, width), BF16), pltpu.VMEM((n_pg * page, width), BF16),
                        pltpu.VMEM((n_rows, 1), F32), pltpu.VMEM((n_rows, 1), F32),
                        pltpu.VMEM((n_pair, rp, wp), F32)],
    )
    return pl.pallas_call(
        kern,
        out_shape=jax.ShapeDtypeStruct((n_seq, n_t, width), F32),
        grid_spec=grid_spec,
        compiler_params=_cparams(("arbitrary", "arbitrary")),
        name="diff_decode",
    )(page_table, *([pool_k] * n_pg), *([pool_v] * n_pg), qbd, k_new, v_new, b_far, b_last, b_new,
      lam_vecs, subln_g)


def _mla_decode_kernel(pt_ref, *refs, n_pg, page, n_t):
    del pt_ref
    c_pages = refs[:n_pg]
    r_pages = refs[n_pg:2 * n_pg]
    (ql_ref, qp_ref, cn_ref, rn_ref, o_ref, cc_ref, rc_ref, m_ref, l_ref, acc_ref) = refs[2 * n_pg:]
    j = pl.program_id(1)
    last = pl.num_programs(1) - 1

    @pl.when(j == 0)
    def _init():
        m_ref[...] = jnp.full(m_ref.shape, NEG_INF, F32)
        l_ref[...] = jnp.zeros(l_ref.shape, F32)
        acc_ref[...] = jnp.zeros(acc_ref.shape, F32)

    def attend(n_keys, mask):
        ckv = cc_ref[0:n_keys, :]
        s = _dot_nt(ql_ref[...], ckv) + _dot_nt(qp_ref[...], rc_ref[0:n_keys, :])
        if mask is not None:
            s = jnp.where(mask, s, NEG_INF)
        alpha, p = _online_update(s, m_ref, l_ref)
        acc_ref[...] = alpha * acc_ref[...] + _dot(p, ckv)

    for i in range(n_pg):
        cc_ref[i * page:(i + 1) * page, :] = c_pages[i][...].astype(BF16)
        rc_ref[i * page:(i + 1) * page, :] = r_pages[i][...].astype(BF16)
    attend(n_pg * page, None)

    @pl.when(j == last)
    def _finish():
        cc_ref[0:LANES, :] = jnp.zeros((LANES, cc_ref.shape[1]), BF16)
        rc_ref[0:LANES, :] = jnp.zeros((LANES, rc_ref.shape[1]), BF16)
        cc_ref[0:n_t, :] = cn_ref[...].astype(BF16)
        rc_ref[0:n_t, :] = rn_ref[...].astype(BF16)
        rows = ql_ref.shape[0]
        r = lax.broadcasted_iota(jnp.int32, (rows, LANES), 0)
        c = lax.broadcasted_iota(jnp.int32, (rows, LANES), 1)
        attend(LANES, c <= (r % n_t))
        o_ref[...] = (acc_ref[...] / l_ref[...]).astype(o_ref.dtype)


def _mla_decode(page_table, pool_c, pool_r, q_lat, q_pe, c_new, r_new, *, n_pg):
    n_seq, n_pages = page_table.shape
    _, page, dc = pool_c.shape
    dr = pool_r.shape[2]
    rows = q_lat.shape[1]
    n_t = c_new.shape[1]
    n_chunk = n_pages // n_pg

    def page_spec(i, width):
        return pl.BlockSpec((None, page, width), lambda s, j, pt: (pt[s, j * n_pg + i], 0, 0))

    seq_spec = lambda shape: pl.BlockSpec((None,) + shape, lambda s, j, pt: (s,) + (0,) * len(shape))
    kern = functools.partial(_mla_decode_kernel, n_pg=n_pg, page=page, n_t=n_t)
    grid_spec = pltpu.PrefetchScalarGridSpec(
        num_scalar_prefetch=1,
        grid=(n_seq, n_chunk),
        in_specs=([page_spec(i, dc) for i in range(n_pg)] + [page_spec(i, dr) for i in range(n_pg)]
                  + [seq_spec((rows, dc)), seq_spec((rows, dr)), seq_spec((n_t, dc)), seq_spec((n_t, dr))]),
        out_specs=seq_spec((rows, dc)),
        scratch_shapes=[pltpu.VMEM((n_pg * page, dc), BF16), pltpu.VMEM((n_pg * page, dr), BF16),
                        pltpu.VMEM((rows, 1), F32), pltpu.VMEM((rows, 1), F32),
                        pltpu.VMEM((rows, dc), F32)],
    )
    return pl.pallas_call(
        kern,
        out_shape=jax.ShapeDtypeStruct((n_seq, rows, dc), BF16),
        grid_spec=grid_spec,
        compiler_params=_cparams(("arbitrary", "arbitrary")),
        name="mla_decode",
    )(page_table, *([pool_c] * n_pg), *([pool_r] * n_pg), q_lat, q_pe, c_new, r_new)


def _mla_proj_kernel(x_ref, cos_ref, sin_ref, g_ref, w1_ref, qg_ref, kg_ref, wqn_ref, wqp_ref, wqr_ref,
                     wuk_ref, wuv_ref, *out_refs, sample, n_heads, qscale, lora):
    cosp = cos_ref[...]
    sinp = sin_ref[...]
    hn = _rms(x_ref[...], g_ref[...]).astype(BF16)
    a = _dot(hn, w1_ref[...])
    cqn = _rms(a[:, :lora], qg_ref[...]).astype(BF16)
    ckvn = _rms(a[:, lora:2 * lora], kg_ref[...])
    kp = a[:, 2 * lora:2 * lora + LANES] * cosp + a[:, 2 * lora + LANES:] * sinp
    rope_w = kp.shape[1] // 2
    ckvb = ckvn.astype(BF16)
    qn = _dot(cqn, wqn_ref[...])
    qp = _dot(cqn, wqp_ref[...])
    qr = _dot(cqn, wqr_ref[...])
    if sample:
        ckv_ref, kpe_ref, ql_ref, qpe_ref = out_refs
    else:
        ckv_ref, kpe_ref, q_ref, kc_ref, v_ref = out_refs
        kn = _dot(ckvb, wuk_ref[...])
        v_ref[...] = _dot(ckvb, wuv_ref[...]).astype(BF16)
    ckv_ref[...] = ckvn
    kpe_ref[...] = kp[:, :rope_w]
    for h in range(n_heads):
        hs = slice(h * LANES, (h + 1) * LANES)
        q_nope = qn[:, hs]
        q_rope = (qp[:, hs] * cosp + qr[:, hs] * sinp) * qscale
        if sample:
            ql_ref[h] = (_dot_nt(q_nope.astype(BF16), wuk_ref[:, hs]) * qscale).astype(BF16)
            qpe_ref[h] = q_rope[:, :rope_w].astype(BF16)
        else:
            q_ref[h, :, 0:LANES] = (q_nope * qscale).astype(BF16)
            q_ref[h, :, LANES:] = q_rope.astype(BF16)
            kc_ref[h, :, 0:LANES] = kn[:, hs].astype(BF16)
            kc_ref[h, :, LANES:] = kp.astype(BF16)


def _mla_proj(x, cosp, sinp, g, w1, qg, kg, wqn, wqp, wqr, wuk, wuv, *, sample, qscale, pos_blocks):
    m, d = x.shape
    lora = qg.shape[1]
    n_heads = wqn.shape[1] // LANES
    rope_w = LANES // 2
    tm = _row_tile(m, 512)
    assert cosp.shape[0] % tm == 0
    n_pos = cosp.shape[0] // tm
    row = lambda width: pl.BlockSpec((tm, width), lambda i: (i, 0))
    hrow = lambda width: pl.BlockSpec((n_heads, tm, width), lambda i: (0, i, 0))
    pos_spec = pl.BlockSpec((tm, LANES), lambda i: (i % n_pos, 0))
    if sample:
        out_shape = (jax.ShapeDtypeStruct((m, lora), F32), jax.ShapeDtypeStruct((m, rope_w), F32),
                     jax.ShapeDtypeStruct((n_heads, m, lora), BF16),
                     jax.ShapeDtypeStruct((n_heads, m, rope_w), BF16))
        out_specs = (row(lora), row(rope_w), hrow(lora), hrow(rope_w))
    else:
        out_shape = (jax.ShapeDtypeStruct((m, lora), F32), jax.ShapeDtypeStruct((m, rope_w), F32),
                     jax.ShapeDtypeStruct((n_heads, m, 2 * LANES), BF16),
                     jax.ShapeDtypeStruct((n_heads, m, 2 * LANES), BF16),
                     jax.ShapeDtypeStruct((m, n_heads * LANES), BF16))
        out_specs = (row(lora), row(rope_w), hrow(2 * LANES), hrow(2 * LANES), row(n_heads * LANES))
    del pos_blocks
    kern = functools.partial(_mla_proj_kernel, sample=sample, n_heads=n_heads, qscale=qscale, lora=lora)
    return pl.pallas_call(
        kern,
        out_shape=out_shape,
        grid=(m // tm,),
        in_specs=[row(d), pos_spec, pos_spec, _const_spec(g.shape), _const_spec(w1.shape),
                  _const_spec(qg.shape), _const_spec(kg.shape), _const_spec(wqn.shape),
                  _const_spec(wqp.shape), _const_spec(wqr.shape), _const_spec(wuk.shape),
                  _const_spec(wuv.shape)],
        out_specs=out_specs,
        compiler_params=_cparams(("arbitrary",)),
        name="mla_proj_sample" if sample else "mla_proj_prompt",
    )(x, cosp, sinp, g, w1, qg, kg, wqn, wqp, wqr, wuk, wuv)


def _head_matmul_kernel(o_ref, w_ref, y_ref):
    y_ref[...] = _dot(o_ref[...], w_ref[...]).astype(y_ref.dtype)


def _head_matmul(o, w):
    n_heads, m, c = o.shape
    v = w.shape[2]
    return pl.pallas_call(
        _head_matmul_kernel,
        out_shape=jax.ShapeDtypeStruct((m, n_heads * v), BF16),
        grid=(n_heads,),
        in_specs=[pl.BlockSpec((None, m, c), lambda h: (h, 0, 0)),
                  pl.BlockSpec((None, c, v), lambda h: (h, 0, 0))],
        out_specs=pl.BlockSpec((m, v), lambda h: (0, h)),
        compiler_params=_cparams(("arbitrary",)),
        name="mla_value_up",
    )(o, w)


def _proj_res_kernel(o_ref, x_ref, w_ref, g_ref, y_ref):
    y = _dot(o_ref[...].astype(BF16), w_ref[...])
    y_ref[...] = x_ref[...] + _rms(y, g_ref[...])


def _proj_res(o, x, w, g):
    m, d = x.shape
    k = o.shape[1]
    tm = _row_tile(m, 512)
    return pl.pallas_call(
        _proj_res_kernel,
        out_shape=jax.ShapeDtypeStruct((m, d), F32),
        grid=(m // tm,),
        in_specs=[pl.BlockSpec((tm, k), lambda i: (i, 0)), pl.BlockSpec((tm, d), lambda i: (i, 0)),
                  _const_spec(w.shape), _const_spec(g.shape)],
        out_specs=pl.BlockSpec((tm, d), lambda i: (i, 0)),
        compiler_params=_cparams(("arbitrary",)),
        name="attn_out_proj",
    )(o, x, w, g)


def _ffn_kernel(x_ref, g2_ref, g3_ref, win_ref, cw_ref, cb_ref, wout_ref, st_ref, y_ref, cs_ref,
                ext_ref, h_ref, *, tm, ff, shift, pad, chunk):
    i = pl.program_id(1)
    lo = pad - 2 * shift

    @pl.when(i == 0)
    def _load_state():
        ext_ref[lo:pad, :] = st_ref[...]

    x = x_ref[...]
    xn = _rms(x, g2_ref[...]).astype(BF16)
    c0 = math.sqrt(2.0 / math.pi)
    for c in range(ff // chunk):
        cs = slice(c * chunk, (c + 1) * chunk)
        gate = _dot(xn, win_ref[:, c * chunk:(c + 1) * chunk])
        val = _dot(xn, win_ref[:, ff + c * chunk:ff + (c + 1) * chunk])
        ext_ref[pad:pad + tm, cs] = gate
        conv = (cb_ref[:, cs] + ext_ref[lo:lo + tm, cs] * cw_ref[0:1, cs]
                + ext_ref[lo + shift:lo + shift + tm, cs] * cw_ref[1:2, cs] + gate * cw_ref[2:3, cs])
        act = 0.5 * conv * (1.0 + jnp.tanh(c0 * (conv + 0.044715 * (conv * conv * conv))))
        h_ref[:, cs] = (act * val).astype(BF16)
        tail = ext_ref[tm + lo:tm + pad, cs]
        cs_ref[:, cs] = tail
        ext_ref[lo:pad, cs] = tail
    y = _dot(h_ref[...], wout_ref[...])
    y_ref[...] = x + _rms(y, g3_ref[...])


def _conv_ffn(x, g2, g3, w_in, conv_w, conv_b, w_out, state, *, n_seq, shift, tm_cap):
    m, d = x.shape
    ff = w_out.shape[0]
    seq_len = m // n_seq
    tm = _row_tile(seq_len, tm_cap)
    nb = seq_len // tm
    pad = max(8, 2 * shift)
    chunk = 256
    assert ff % chunk == 0 and tm >= 2 * shift
    kern = functools.partial(_ffn_kernel, tm=tm, ff=ff, shift=shift, pad=pad, chunk=chunk)
    const = lambda shape: pl.BlockSpec(shape, lambda s, i: (0,) * len(shape), pipeline_mode=pl.Buffered(1))
    return pl.pallas_call(
        kern,
        out_shape=(jax.ShapeDtypeStruct((m, d), F32), jax.ShapeDtypeStruct((n_seq, 2 * shift, ff), F32)),
        grid=(n_seq, nb),
        in_specs=[pl.BlockSpec((tm, d), lambda s, i: (s * nb + i, 0)), const(g2.shape), const(g3.shape),
                  const(w_in.shape), const(conv_w.shape), const(conv_b.shape), const(w_out.shape),
                  pl.BlockSpec((None, 2 * shift, ff), lambda s, i: (s, 0, 0))],
        out_specs=(pl.BlockSpec((tm, d), lambda s, i: (s * nb + i, 0)),
                   pl.BlockSpec((None, 2 * shift, ff), lambda s, i: (s, 0, 0))),
        scratch_shapes=[pltpu.VMEM((pad + tm, ff), F32), pltpu.VMEM((tm, ff), BF16)],
        compiler_params=_cparams(("arbitrary", "arbitrary")),
        name="conv_ffn",
    )(x, g2, g3, w_in, conv_w, conv_b, w_out, state)


def _rope_tables(pos):
    half = 32
    inv = jnp.power(ROPE_THETA, -jnp.arange(half, dtype=F32) / half)
    ang = pos.astype(F32)[:, None] * inv
    z = jnp.zeros((pos.shape[0], LANES - 2 * half), F32)
    cos, sin = jnp.cos(ang), jnp.sin(ang)
    return jnp.concatenate([cos, cos, z], axis=1), jnp.concatenate([sin, sin, z], axis=1)


def _rot_cols(w):
    half = w.shape[-1] // 2
    return jnp.concatenate([-w[..., half:], w[..., :half]], axis=-1)


def _pad_cols(w, width):
    return jnp.concatenate([w, jnp.zeros(w.shape[:-1] + (width - w.shape[-1],), w.dtype)], axis=-1)


def _pad_rows(a, rows):
    return jnp.concatenate([a, jnp.zeros((rows - a.shape[0],) + a.shape[1:], a.dtype)], axis=0)


def kernel(x_prompt, x_sample, cache_k_a, cache_v_a, cache_ckv_b, cache_kpe_b, state_conv, page_table,
           meta_tokens, rel_bias, norm_g, w_qkv_a, lambda_q1, lambda_k1, lambda_q2, lambda_k2, subln_g, w_o_a,
           w_dqkv_b, q_norm_g, w_uq, kv_norm_g, w_uk, w_uv, w_o_b, w_ffn_in, conv_w, conv_b, w_ffn_out):
    n_b, seq, d = x_prompt.shape
    n_s, n_t, _ = x_sample.shape
    n_meta = meta_tokens.shape[0]
    depth = norm_g.shape[0]
    ff = w_ffn_out.shape[1]
    a_heads, a_dh = cache_k_a.shape[3], cache_k_a.shape[5]
    a_w = a_heads * 2 * a_dh
    n_pool, page = cache_k_a.shape[1], cache_k_a.shape[2]
    n_pages = page_table.shape[1]
    past = n_pages * page
    b_heads, b_nope = w_uk.shape[2], w_uk.shape[3]
    lora_q, lora_kv, b_rope = q_norm_g.shape[1], kv_norm_g.shape[1], cache_kpe_b.shape[3]
    mla_scale = (b_nope + b_rope) ** -0.5
    T = ATTN_TILE
    assert seq % T == 0 and n_meta <= LANES and n_t <= 8 and b_nope == LANES and 2 * a_dh == LANES
    assert lora_q == lora_kv and b_rope == LANES // 2

    xm = meta_tokens.astype(F32)
    xp = x_prompt.reshape(n_b * seq, d)
    xs = x_sample.transpose(1, 0, 2).reshape(n_t * n_s, d)

    pos_m = jnp.arange(n_meta, dtype=jnp.int32)
    pos_p = n_meta + jnp.arange(seq, dtype=jnp.int32)
    pos_s = jnp.repeat(past + jnp.arange(n_t, dtype=jnp.int32), n_s)
    rope_m, rope_p, rope_s = _rope_tables(pos_m), _rope_tables(pos_p), _rope_tables(pos_s)

    def to_seq_major(a):
        return a.reshape(n_t, n_s, a.shape[-1]).transpose(1, 0, 2)

    def with_meta(meta_rows, main_rows):
        w = meta_rows.shape[-1]
        return jnp.concatenate([jnp.broadcast_to(meta_rows[None], (n_b, n_meta, w)),
                                main_rows.reshape(n_b, seq, w)], axis=1)

    outs = {k: [] for k in ("k_a_p", "v_a_p", "k_a_s", "v_a_s", "ckv_p", "kpe_p", "ckv_s", "kpe_s",
                            "conv_p", "conv_s")}
    for i in range(depth):
        g = norm_g[i].astype(F32)
        g0, g1, g2, g3 = (g[r:r + 1] for r in range(4))
        j = i // 2
        if i % 2 == 0:
            lam_init = 0.8 - 0.6 * math.exp(-0.3 * i)
            lam_vecs = jnp.stack([lambda_q1[j], lambda_k1[j], lambda_q2[j], lambda_k2[j]]).astype(F32)
            sg = subln_g[j].astype(F32)[None]
            w_qkv = w_qkv_a[j].astype(BF16)
            qscale = a_dh ** -0.5 * LOG2E
            b31 = rel_bias[N_BUCKETS - 1].astype(F32) * LOG2E
            qm, km, vm, kmb, vmb = _qkv_proj(xm, g0, w_qkv, qscale)
            qp, kp, vp, kpb, vpb = _qkv_proj(xp, g0, w_qkv, qscale)
            qs, ks, vs, _, _ = _qkv_proj(xs, g0, w_qkv, qscale)
            kmb, vmb = _pad_rows(kmb, LANES), _pad_rows(vmb, LANES)
            bias_meta = _bias_tiles(rel_bias, n_meta, ((0, LANES),))
            bias_main = _bias_tiles(rel_bias, T, ((-T, T), (0, T), (-n_meta, LANES)))
            om = _flash_diff(qm, kmb[:n_meta], vmb[:n_meta], kmb, vmb, bias_meta, b31, lam_vecs, sg, n_seq=1,
                             n_heads=a_heads, has_main=False, n_meta=n_meta, lam_init=lam_init,
                             bias_meta_off=0)
            op = _flash_diff(qp, kpb, vpb, kmb, vmb, bias_main, b31, lam_vecs, sg, n_seq=n_b,
                             n_heads=a_heads, has_main=True, n_meta=n_meta, lam_init=lam_init,
                             bias_meta_off=2 * T)
            n_pg = 8
            n_pair = a_heads // 2
            bias_dec = _bias_tiles(rel_bias, 8, ((-n_pg * page, n_pg * page), (0, LANES)))[:, :n_t]
            expand = lambda b: jnp.broadcast_to(b[:, None], (a_heads, 2, n_t, b.shape[-1])).reshape(
                a_heads * 2 * n_t, b.shape[-1])
            b_last = expand(bias_dec[:, :, :n_pg * page])
            col = jnp.arange(LANES)[None, None, :]
            b_new = expand(jnp.where((col < n_t) & (col <= jnp.arange(n_t)[None, :, None]),
                                     bias_dec[:, :, n_pg * page:], NEG_INF))
            b_far = jnp.repeat(b31, 2 * n_t)[:, None]
            q5 = to_seq_major(qs).reshape(n_s, n_t, n_pair, 4, a_dh)
            eye = jnp.eye(4, dtype=BF16)
            qbd = (q5.transpose(0, 2, 3, 1, 4)[:, :, :, :, None, :]
                   * eye[None, None, :, None, :, None]).reshape(n_s, n_pair, 4 * n_t, 4 * a_dh)
            os_ = _diff_decode(page_table, cache_k_a[j].reshape(n_pool, page, a_w),
                               cache_v_a[j].reshape(n_pool, page, a_w), qbd, to_seq_major(ks),
                               to_seq_major(vs), b_far, b_last, b_new, lam_vecs, sg, n_pg=n_pg,
                               lam_init=lam_init)
            os_ = os_.transpose(1, 0, 2).reshape(n_t * n_s, a_w)
            w_o = w_o_a[j].astype(BF16)
            outs["k_a_p"].append(with_meta(km, kp).reshape(n_b, n_meta + seq, a_heads, 2, a_dh))
            outs["v_a_p"].append(with_meta(vm, vp).reshape(n_b, n_meta + seq, a_heads, 2 * a_dh))
            outs["k_a_s"].append(to_seq_major(ks).reshape(n_s, n_t, a_heads, 2, a_dh))
            outs["v_a_s"].append(to_seq_major(vs).reshape(n_s, n_t, a_heads, 2 * a_dh))
        else:
            wd = w_dqkv_b[j]
            w_kpe = wd[:, lora_q + lora_kv:]
            w1 = jnp.concatenate([wd[:, :lora_q + lora_kv], _pad_cols(w_kpe, LANES),
                                  _pad_cols(_rot_cols(w_kpe), LANES)], axis=1).astype(BF16)
            wq = w_uq[j].reshape(lora_q, b_heads, b_nope + b_rope)
            wqn = wq[:, :, :b_nope].reshape(lora_q, b_heads * b_nope).astype(BF16)
            wqp = _pad_cols(wq[:, :, b_nope:], LANES).reshape(lora_q, b_heads * LANES).astype(BF16)
            wqr = _pad_cols(_rot_cols(wq[:, :, b_nope:]), LANES).reshape(lora_q, b_heads * LANES).astype(BF16)
            wuk = w_uk[j].reshape(lora_kv, b_heads * b_nope).astype(BF16)
            wuv = w_uv[j].reshape(lora_kv, -1).astype(BF16)
            qg, kg = q_norm_g[j].astype(F32)[None], kv_norm_g[j].astype(F32)[None]
            qscale = mla_scale * LOG2E
            proj = functools.partial(_mla_proj, g=g0, w1=w1, qg=qg, kg=kg, wqn=wqn, wqp=wqp, wqr=wqr,
                                     wuk=wuk, wuv=wuv, qscale=qscale, pos_blocks=None)
            ckm, kpm, qm, kcm, vmb = proj(xm, *rope_m, sample=False)
            ckp, kpp, qp, kcp, vpb = proj(xp, *rope_p, sample=False)
            cks, kps, qls, qps = proj(xs, *rope_s, sample=True)
            kcm_pad = jnp.concatenate([kcm, jnp.zeros((b_heads, LANES - n_meta, kcm.shape[2]), BF16)], axis=1)
            vmb_pad = _pad_rows(vmb, LANES)
            om = _flash_mla(qm, kcm, vmb, kcm_pad, vmb_pad, n_seq=1, has_main=False, n_meta=n_meta)
            op = _flash_mla(qp, kcp, vpb, kcm_pad, vmb_pad, n_seq=n_b, has_main=True, n_meta=n_meta)
            n_pg = 16
            hseq = lambda a: a.reshape(b_heads, n_t, n_s, a.shape[-1]).transpose(2, 0, 1, 3).reshape(
                n_s, b_heads * n_t, a.shape[-1])
            ol = _mla_decode(page_table, cache_ckv_b[j], cache_kpe_b[j], hseq(qls), hseq(qps),
                             to_seq_major(cks), to_seq_major(kps), n_pg=n_pg)
            ol = ol.reshape(n_s, b_heads, n_t, lora_kv).transpose(1, 2, 0, 3).reshape(
                b_heads, n_t * n_s, lora_kv)
            os_ = _head_matmul(ol, w_uv[j].transpose(1, 0, 2).astype(BF16))
            w_o = w_o_b[j].astype(BF16)
            outs["ckv_p"].append(with_meta(ckm, ckp))
            outs["kpe_p"].append(with_meta(kpm, kpp))
            outs["ckv_s"].append(to_seq_major(cks))
            outs["kpe_s"].append(to_seq_major(kps))
        xm = _proj_res(om, xm, w_o, g1)
        xp = _proj_res(op, xp, w_o, g1)
        xs = _proj_res(os_, xs, w_o, g1)

        w_in, w_out = w_ffn_in[i].astype(BF16), w_ffn_out[i].astype(BF16)
        cw, cb = conv_w[i].astype(F32), conv_b[i].astype(F32)[None]
        ffn = functools.partial(_conv_ffn, g2=g2, g3=g3, w_in=w_in, conv_w=cw, conv_b=cb, w_out=w_out)
        xm, cm = ffn(xm, state=jnp.zeros((1, 2, ff), F32), n_seq=1, shift=1, tm_cap=512)
        xp, cp = ffn(xp, state=jnp.broadcast_to(cm, (n_b, 2, ff)), n_seq=n_b, shift=1, tm_cap=512)
        st = state_conv[i].astype(F32).transpose(1, 0, 2).reshape(1, 2 * n_s, ff)
        xs, cs = ffn(xs, state=st, n_seq=1, shift=n_s, tm_cap=2 * n_s)
        outs["conv_p"].append(cp)
        outs["conv_s"].append(cs.reshape(2, n_s, ff).transpose(1, 0, 2))

    y_prompt = xp.reshape(n_b, seq, d)
    y_sample = to_seq_major(xs)
    return (y_prompt, y_sample,
            jnp.stack(outs["k_a_p"]), jnp.stack(outs["v_a_p"]), jnp.stack(outs["k_a_s"]),
            jnp.stack(outs["v_a_s"]), jnp.stack(outs["ckv_p"]), jnp.stack(outs["kpe_p"]),
            jnp.stack(outs["ckv_s"]), jnp.stack(outs["kpe_s"]), jnp.stack(outs["conv_p"]),
            jnp.stack(outs["conv_s"]))
```

```python
import functools
import math

import jax
import jax.numpy as jnp
from jax import lax
from jax.experimental import pallas as pl
from jax.experimental.pallas import tpu as pltpu

F32 = jnp.float32
BF16 = jnp.bfloat16

EPS = 1e-6
NEG_INF = -1e30
LOG2E = 1.4426950408889634
N_BUCKETS = 32
MAX_DISTANCE = 128
ROPE_THETA = 10000.0
LANES = 128
ATTN_TILE = 256
VMEM_LIMIT = 56 * 1024 * 1024


def _cparams(sem):
    return pltpu.CompilerParams(dimension_semantics=sem, vmem_limit_bytes=VMEM_LIMIT)


def _const_spec(shape):
    nd = len(shape)
    return pl.BlockSpec(shape, lambda *_: (0,) * nd, pipeline_mode=pl.Buffered(1))


def _rms(x, g):
    return x * lax.rsqrt(jnp.mean(x * x, axis=-1, keepdims=True) + EPS) * g


def _dot(a, b):
    return jnp.dot(a, b, preferred_element_type=F32)


def _dot_nt(a, b):
    return lax.dot_general(a, b, (((1,), (1,)), ((), ())), preferred_element_type=F32)


def _row_tile(m, cap):
    t = min(m, cap)
    while m % t:
        t //= 2
    return t


def _lambda(lam_ref, lam_init):
    lv = lam_ref[...]
    return (jnp.exp(jnp.sum(lv[0:1] * lv[1:2], axis=-1, keepdims=True))
            - jnp.exp(jnp.sum(lv[2:3] * lv[3:4], axis=-1, keepdims=True)) + lam_init)


def _bias_kernel(tbl_ref, o_ref, *, segs, keys_on_rows):
    h = pl.program_id(0)
    rows = o_ref.shape[0]
    col0 = 0
    max_exact = N_BUCKETS // 2
    for off, width in segs:
        r = lax.broadcasted_iota(jnp.int32, (rows, width), 0)
        c = lax.broadcasted_iota(jnp.int32, (rows, width), 1)
        rel = (r - c + off) if keys_on_rows else (c - r + off)
        n = jnp.maximum(-rel, 0)
        nf = jnp.maximum(n, 1).astype(F32)
        log_b = max_exact + (jnp.log(nf / max_exact) / math.log(MAX_DISTANCE / max_exact)
                             * (N_BUCKETS - max_exact)).astype(jnp.int32)
        bucket = jnp.where(n < max_exact, n, jnp.minimum(log_b, N_BUCKETS - 1))
        val = jnp.full((rows, width), tbl_ref[N_BUCKETS - 1, h], F32)
        for b in range(N_BUCKETS - 1):
            val = jnp.where(bucket == b, tbl_ref[b, h], val)
        o_ref[:, col0:col0 + width] = val * LOG2E
        col0 += width


def _bias_tiles(rel_bias, rows, segs, keys_on_rows):
    n_heads = rel_bias.shape[1]
    width = sum(w for _, w in segs)
    return pl.pallas_call(
        functools.partial(_bias_kernel, segs=segs, keys_on_rows=keys_on_rows),
        out_shape=jax.ShapeDtypeStruct((n_heads, rows, width), F32),
        grid=(n_heads,),
        in_specs=[pl.BlockSpec(memory_space=pltpu.SMEM)],
        out_specs=pl.BlockSpec((None, rows, width), lambda h: (h, 0, 0)),
        compiler_params=_cparams(("arbitrary",)),
        name="rel_bias_tiles",
    )(rel_bias)


def _qkv_kernel(x_ref, g_ref, w_ref, wt_ref, *out_refs, qscale, transposed):
    xn = _rms(x_ref[...], g_ref[...]).astype(BF16)
    if transposed:
        k_ref, v_ref, kb_ref, qt_ref, vt_ref = out_refs
    else:
        k_ref, v_ref, q_ref = out_refs
    n = k_ref.shape[1]
    k = _dot(xn, w_ref[:, n:2 * n])
    k_ref[...] = k
    v = _dot(xn, w_ref[:, 2 * n:])
    v_ref[...] = v
    if transposed:
        kb_ref[...] = k.astype(BF16)
        qt = (_dot_nt(wt_ref[0:n, :], xn) * qscale).astype(BF16)
        vt = _dot_nt(wt_ref[n:2 * n, :], xn).astype(BF16)
        for ref, val in ((qt_ref, qt), (vt_ref, vt)):
            tile = ref.shape[2]
            for t in range(ref.shape[0]):
                ref[t] = val[:, t * tile:(t + 1) * tile]
    else:
        q_ref[...] = (_dot(xn, w_ref[:, :n]) * qscale).astype(BF16)


def _qkv_proj(x, g, w, wt, qscale, *, transposed, q_tile, v_tile):
    m, d = x.shape
    n = w.shape[1] // 3
    tm = _row_tile(m, 512)
    row = lambda width: pl.BlockSpec((tm, width), lambda i: (i, 0))

    def t_out(tile):
        tile = min(tile, tm)
        return (jax.ShapeDtypeStruct((m // tile, n, tile), BF16),
                pl.BlockSpec((tm // tile, n, tile), lambda i: (i, 0, 0)))

    if transposed:
        (q_shape, q_spec), (v_shape, v_spec) = t_out(q_tile), t_out(v_tile)
        out_shape = (jax.ShapeDtypeStruct((m, n), F32), jax.ShapeDtypeStruct((m, n), F32),
                     jax.ShapeDtypeStruct((m, n), BF16), q_shape, v_shape)
        out_specs = (row(n), row(n), row(n), q_spec, v_spec)
    else:
        out_shape = (jax.ShapeDtypeStruct((m, n), F32), jax.ShapeDtypeStruct((m, n), F32),
                     jax.ShapeDtypeStruct((m, n), BF16))
        out_specs = (row(n), row(n), row(n))
    return pl.pallas_call(
        functools.partial(_qkv_kernel, qscale=qscale, transposed=transposed),
        out_shape=out_shape,
        grid=(m // tm,),
        in_specs=[row(d), _const_spec((1, d)), _const_spec(w.shape), _const_spec(wt.shape)],
        out_specs=out_specs,
        compiler_params=_cparams(("arbitrary",)),
        name="qkv_proj_t" if transposed else "qkv_proj",
    )(x, g, w, wt)


def _flash_kernel(*refs, diff, has_main, tq, n_meta, lam_init):
    if diff:
        (b31_ref, lam_ref, sg_ref, qt_ref, k_ref, vt_ref, km_ref, vmt_ref, bias_ref,
         o_ref, acc_ref, q2_ref) = refs
    else:
        qt_ref, k_ref, vt_ref, km_ref, vmt_ref, o_ref, acc_ref = refs
    h = pl.program_id(1)
    qi = pl.program_id(2)
    T = ATTN_TILE
    nq = acc_ref.shape[1]

    if diff:
        qt = qt_ref[...]
        row = lax.broadcasted_iota(jnp.int32, qt.shape, 0)
        half = qt.shape[0] // 2
        q2_ref[:, :tq] = jnp.where(row < half, qt, jnp.zeros_like(qt))
        q2_ref[:, tq:] = jnp.where(row >= half, qt, jnp.zeros_like(qt))
        qq = q2_ref[...]
        b31 = b31_ref[h]
    else:
        qq = qt_ref[...]
        b31 = None
    both = (lambda a: jnp.concatenate([a, a], axis=1)) if diff else (lambda a: a)
    acc_ref[...] = jnp.zeros(acc_ref.shape, F32)

    def tile(kt, vtt, bias, mask, carry):
        m_prev, l_prev = carry
        s = _dot(kt, qq)
        if bias is not None:
            s = s + bias
        if mask is not None:
            s = jnp.where(mask, s, NEG_INF)
        m_new = jnp.maximum(m_prev, jnp.max(s, axis=0, keepdims=True))
        alpha = jnp.exp2(m_prev - m_new)
        p = jnp.exp2(s - m_new)
        l_new = alpha * l_prev + jnp.sum(p, axis=0, keepdims=True)
        acc_ref[...] = alpha * acc_ref[...] + _dot(vtt, p.astype(BF16))
        return m_new, l_new

    carry = (jnp.full((1, nq), NEG_INF, F32), jnp.zeros((1, nq), F32))

    r = lax.broadcasted_iota(jnp.int32, (LANES, tq), 0)
    c = lax.broadcasted_iota(jnp.int32, (LANES, tq), 1)
    n_diag = tq // T
    if has_main:
        meta_mask = r < n_meta
        lane0 = (1 + n_diag) * tq
        meta_bias = both(jnp.where(qi == 0, bias_ref[0:LANES, lane0:lane0 + tq], b31)) if diff else None
    else:
        meta_mask = (r < n_meta) & (r <= c)
        meta_bias = both(bias_ref[...]) if diff else None
    carry = tile(km_ref[...], vmt_ref[...], meta_bias, both(meta_mask), carry)

    if has_main:
        tk_far = vt_ref.shape[2]
        fsub = tk_far // T
        n_prev = 1 if diff else 0
        assert (n_diag - n_prev) % fsub == 0 or fsub == 1
        n_far = (qi * n_diag - n_prev) // fsub

        def far_body(kt_i, carry):
            off = pl.multiple_of(kt_i * tk_far, tk_far)
            return tile(k_ref[pl.ds(off, tk_far), :], vt_ref[kt_i], b31, None, carry)
        carry = lax.fori_loop(0, jnp.maximum(n_far, 0), far_body, carry)

        if diff:
            kt_i = jnp.maximum(qi * n_diag - 1, 0)
            off = pl.multiple_of(kt_i * T, T)
            prev_mask = jnp.broadcast_to(qi >= 1, (T, nq))
            carry = tile(k_ref[pl.ds(off, T), :], vt_ref[kt_i], both(bias_ref[:, 0:tq]), prev_mask, carry)

        rr = lax.broadcasted_iota(jnp.int32, (T, tq), 0)
        cc = lax.broadcasted_iota(jnp.int32, (T, tq), 1)
        for dgl in range(n_diag):
            off = pl.multiple_of((qi * n_diag + dgl) * T, T)
            sub = dgl % fsub
            vtt = vt_ref[qi * (n_diag // fsub) + dgl // fsub, :, sub * T:(sub + 1) * T]
            bias = both(bias_ref[:, (1 + dgl) * tq:(2 + dgl) * tq]) if diff else None
            carry = tile(k_ref[pl.ds(off, T), :], vtt, bias, both(rr + dgl * T <= cc), carry)

    _, l_fin = carry
    a = acc_ref[...] / l_fin
    if diff:
        o = a[:, :tq] - _lambda(lam_ref, lam_init) * a[:, tq:]
        o = o * lax.rsqrt(jnp.mean(o * o, axis=0, keepdims=True) + EPS) * sg_ref[...] * (1.0 - lam_init)
    else:
        o = a
    o_ref[...] = o.T.astype(o_ref.dtype)


def _flash_diff(qt, kb, vt, km, vmt, bias, b31, lam_vecs, sg_col, *, n_seq, n_heads, has_main, n_meta,
                lam_init):
    hd = kb.shape[1]
    dv = hd // n_heads
    tq, tkv = qt.shape[2], vt.shape[2]
    seq_len = qt.shape[0] * tq // n_seq
    nq = seq_len // tq
    kern = functools.partial(_flash_kernel, diff=True, has_main=has_main, tq=tq, n_meta=n_meta,
                             lam_init=lam_init)
    if has_main:
        kspec = pl.BlockSpec((seq_len, dv), lambda b, h, i: (b, h))
        vspec = pl.BlockSpec((seq_len // tkv, dv, tkv), lambda b, h, i: (b, h, 0))
    else:
        kspec = pl.BlockSpec((LANES, dv), lambda b, h, i: (0, h))
        vspec = pl.BlockSpec((1, dv, LANES), lambda b, h, i: (0, h, 0))
    return pl.pallas_call(
        kern,
        out_shape=jax.ShapeDtypeStruct((n_seq * seq_len, hd), BF16),
        grid=(n_seq, n_heads, nq),
        in_specs=[pl.BlockSpec(memory_space=pltpu.SMEM), _const_spec(lam_vecs.shape),
                  _const_spec(sg_col.shape),
                  pl.BlockSpec((None, dv, tq), lambda b, h, i: (b * nq + i, h, 0)),
                  kspec, vspec,
                  pl.BlockSpec((LANES, dv), lambda b, h, i: (0, h)),
                  pl.BlockSpec((dv, LANES), lambda b, h, i: (h, 0)),
                  pl.BlockSpec((None,) + bias.shape[1:], lambda b, h, i: (h, 0, 0))],
        out_specs=pl.BlockSpec((tq, dv), lambda b, h, i: (b * nq + i, h)),
        scratch_shapes=[pltpu.VMEM((dv, 2 * tq), F32), pltpu.VMEM((dv, 2 * tq), BF16)],
        compiler_params=_cparams(("arbitrary", "arbitrary", "arbitrary")),
        name="diff_flash" if has_main else "diff_meta_attn",
    )(b31, lam_vecs, sg_col, qt, kb, vt, km, vmt, bias)


def _flash_mla(qt, kc, vt, km, vmt, *, n_seq, has_main, n_meta):
    T = ATTN_TILE
    n_heads, dq, m = qt.shape
    dv = vmt.shape[0] // n_heads
    seq_len = m // n_seq
    tkv = vt.shape[2]
    tq = 2 * T if has_main else seq_len
    nq = seq_len // tq
    kern = functools.partial(_flash_kernel, diff=False, has_main=has_main, tq=tq, n_meta=n_meta,
                             lam_init=0.0)
    if has_main:
        kspec = pl.BlockSpec((None, seq_len, dq), lambda b, h, i: (h, b, 0))
        vspec = pl.BlockSpec((seq_len // tkv, dv, tkv), lambda b, h, i: (b, h, 0))
    else:
        kspec = pl.BlockSpec((None, LANES, dq), lambda b, h, i: (h, 0, 0))
        vspec = pl.BlockSpec((1, dv, LANES), lambda b, h, i: (0, h, 0))
    return pl.pallas_call(
        kern,
        out_shape=jax.ShapeDtypeStruct((m, n_heads * dv), BF16),
        grid=(n_seq, n_heads, nq),
        in_specs=[pl.BlockSpec((None, dq, tq), lambda b, h, i: (h, 0, b * nq + i)),
                  kspec, vspec,
                  pl.BlockSpec((None, LANES, dq), lambda b, h, i: (h, 0, 0)),
                  pl.BlockSpec((dv, LANES), lambda b, h, i: (h, 0))],
        out_specs=pl.BlockSpec((tq, dv), lambda b, h, i: (b * nq + i, h)),
        scratch_shapes=[pltpu.VMEM((dv, tq), F32)],
        compiler_params=_cparams(("arbitrary", "arbitrary", "arbitrary")),
        name="mla_flash" if has_main else "mla_meta_attn",
    )(qt, kc, vt, km, vmt)


def _online_update(s, m_ref, l_ref):
    m_prev = m_ref[...]
    m_new = jnp.maximum(m_prev, jnp.max(s, axis=-1, keepdims=True))
    alpha = jnp.exp2(m_prev - m_new)
    p = jnp.exp2(s - m_new)
    l_ref[...] = alpha * l_ref[...] + jnp.sum(p, axis=-1, keepdims=True)
    m_ref[...] = m_new
    return alpha, p.astype(BF16)


def _diff_decode_kernel(pt_ref, *refs, n_pg, page, n_pair, n_heads, lam_init):
    del pt_ref
    k_pages = refs[:n_pg]
    v_pages = refs[n_pg:2 * n_pg]
    (q_ref, kn_ref, vn_ref, bfar_ref, blast_ref, bnew_ref, lam_ref, sg_ref,
     o_ref, kc_ref, vc_ref, m_ref, l_ref, acc_ref) = refs[2 * n_pg:]
    j = pl.program_id(1)
    last = pl.num_programs(1) - 1
    rp = q_ref.shape[1]
    wp = q_ref.shape[2]
    n_t = kn_ref.shape[1]
    dv = wp // 2

    @pl.when(j == 0)
    def _init():
        m_ref[...] = jnp.full(m_ref.shape, NEG_INF, F32)
        l_ref[...] = jnp.zeros(l_ref.shape, F32)
        acc_ref[...] = jnp.zeros(acc_ref.shape, F32)

    def attend(n_keys, bias):
        s = jnp.concatenate(
            [_dot(q_ref[pr], kc_ref[pr * wp:(pr + 1) * wp, 0:n_keys]) for pr in range(n_pair)], axis=0)
        alpha, p = _online_update(s + bias, m_ref, l_ref)
        for pr in range(n_pair):
            rows = slice(pr * rp, (pr + 1) * rp)
            acc_ref[pr] = alpha[rows] * acc_ref[pr] + _dot(p[rows], vc_ref[0:n_keys, pr * wp:(pr + 1) * wp])

    for i in range(n_pg):
        kc_ref[:, i * page:(i + 1) * page] = k_pages[i][...].astype(BF16)
        for hh in range(n_heads):
            vc_ref[i * page:(i + 1) * page, hh * dv:(hh + 1) * dv] = (
                v_pages[i][pl.ds(hh, page, stride=n_heads), :].astype(BF16))
    attend(n_pg * page, jnp.where(j == last, blast_ref[...], bfar_ref[...]))

    @pl.when(j == last)
    def _finish():
        kc_ref[:, 0:LANES] = jnp.zeros((kc_ref.shape[0], LANES), BF16)
        vc_ref[0:LANES, :] = jnp.zeros((LANES, vc_ref.shape[1]), BF16)
        kc_ref[:, 0:n_t] = kn_ref[...].astype(BF16)
        vc_ref[0:n_t, :] = vn_ref[...].astype(BF16)
        attend(LANES, bnew_ref[...])
        lam = _lambda(lam_ref, lam_init)
        inv_l = 1.0 / l_ref[...]
        for pr in range(n_pair):
            a = acc_ref[pr] * inv_l[pr * rp:(pr + 1) * rp]
            for hl in range(2):
                r0 = hl * 2 * n_t
                o = (a[r0:r0 + n_t, hl * dv:(hl + 1) * dv]
                     - lam * a[r0 + n_t:r0 + 2 * n_t, hl * dv:(hl + 1) * dv])
                o = _rms(o, sg_ref[...]) * (1.0 - lam_init)
                hcol = (2 * pr + hl) * dv
                o_ref[:, hcol:hcol + dv] = o


def _diff_decode(page_table, pool_kt, pool_v, qbd, kt_new, v_new, b_far, b_last, b_new, lam_vecs, subln_g,
                 *, n_pg, n_heads, lam_init):
    n_seq, n_pages = page_table.shape
    _, width, page = pool_kt.shape
    dv = pool_v.shape[2]
    n_pair, rp, wp = qbd.shape[1:]
    n_t = v_new.shape[1]
    n_chunk = n_pages // n_pg
    n_rows = n_pair * rp

    def kpage_spec(i):
        return pl.BlockSpec((None, width, page), lambda s, j, pt: (pt[s, j * n_pg + i], 0, 0))

    def vpage_spec(i):
        return pl.BlockSpec((None, page * n_heads, dv), lambda s, j, pt: (pt[s, j * n_pg + i], 0, 0))

    seq_spec = lambda shape: pl.BlockSpec((None,) + shape, lambda s, j, pt: (s,) + (0,) * len(shape))
    const = lambda shape: pl.BlockSpec(shape, lambda s, j, pt: (0,) * len(shape))
    kern = functools.partial(_diff_decode_kernel, n_pg=n_pg, page=page, n_pair=n_pair, n_heads=n_heads,
                             lam_init=lam_init)
    grid_spec = pltpu.PrefetchScalarGridSpec(
        num_scalar_prefetch=1,
        grid=(n_seq, n_chunk),
        in_specs=([kpage_spec(i) for i in range(n_pg)] + [vpage_spec(i) for i in range(n_pg)]
                  + [seq_spec((n_pair, rp, wp)), seq_spec((width, n_t)), seq_spec((n_t, width)),
                     const(b_far.shape), const(b_last.shape), const(b_new.shape),
                     const(lam_vecs.shape), const(subln_g.shape)]),
        out_specs=seq_spec((n_t, width)),
        scratch_shapes=[pltpu.VMEM((width, n_pg * page), BF16), pltpu.VMEM((n_pg * page, width), BF16),
                        pltpu.VMEM((n_rows, 1), F32), pltpu.VMEM((n_rows, 1), F32),
                        pltpu.VMEM((n_pair, rp, wp), F32)],
    )
    return pl.pallas_call(
        kern,
        out_shape=jax.ShapeDtypeStruct((n_seq, n_t, width), F32),
        grid_spec=grid_spec,
        compiler_params=_cparams(("arbitrary", "arbitrary")),
        name="diff_decode",
    )(page_table, *([pool_kt] * n_pg), *([pool_v] * n_pg), qbd, kt_new, v_new, b_far, b_last, b_new,
      lam_vecs, subln_g)


def _mla_decode_kernel(pt_ref, *refs, n_pg, page, n_t):
    del pt_ref
    c_pages = refs[:n_pg]
    r_pages = refs[n_pg:2 * n_pg]
    (ql_ref, qp_ref, cn_ref, rn_ref, o_ref, cc_ref, rc_ref, m_ref, l_ref, acc_ref) = refs[2 * n_pg:]
    j = pl.program_id(1)
    last = pl.num_programs(1) - 1

    @pl.when(j == 0)
    def _init():
        m_ref[...] = jnp.full(m_ref.shape, NEG_INF, F32)
        l_ref[...] = jnp.zeros(l_ref.shape, F32)
        acc_ref[...] = jnp.zeros(acc_ref.shape, F32)

    def attend(n_keys, mask):
        ckv = cc_ref[0:n_keys, :]
        s = _dot_nt(ql_ref[...], ckv) + _dot(qp_ref[...], rc_ref[:, 0:n_keys])
        if mask is not None:
            s = jnp.where(mask, s, NEG_INF)
        alpha, p = _online_update(s, m_ref, l_ref)
        acc_ref[...] = alpha * acc_ref[...] + _dot(p, ckv)

    for i in range(n_pg):
        cc_ref[i * page:(i + 1) * page, :] = c_pages[i][...].astype(BF16)
        rc_ref[:, i * page:(i + 1) * page] = r_pages[i][...].astype(BF16)
    attend(n_pg * page, None)

    @pl.when(j == last)
    def _finish():
        cc_ref[0:LANES, :] = jnp.zeros((LANES, cc_ref.shape[1]), BF16)
        rc_ref[:, 0:LANES] = jnp.zeros((rc_ref.shape[0], LANES), BF16)
        cc_ref[0:n_t, :] = cn_ref[...].astype(BF16)
        rc_ref[:, 0:n_t] = rn_ref[...].astype(BF16)
        rows = ql_ref.shape[0]
        r = lax.broadcasted_iota(jnp.int32, (rows, LANES), 0)
        c = lax.broadcasted_iota(jnp.int32, (rows, LANES), 1)
        attend(LANES, c <= (r % n_t))
        o_ref[...] = (acc_ref[...] / l_ref[...]).astype(o_ref.dtype)


def _mla_decode(page_table, pool_c, pool_rt, q_lat, q_pe, c_new, rt_new, *, n_pg):
    n_seq, n_pages = page_table.shape
    _, page, dc = pool_c.shape
    dr = pool_rt.shape[1]
    rows = q_lat.shape[1]
    n_t = c_new.shape[1]
    n_chunk = n_pages // n_pg

    def cpage_spec(i):
        return pl.BlockSpec((None, page, dc), lambda s, j, pt: (pt[s, j * n_pg + i], 0, 0))

    def rpage_spec(i):
        return pl.BlockSpec((None, dr, page), lambda s, j, pt: (pt[s, j * n_pg + i], 0, 0))

    seq_spec = lambda shape: pl.BlockSpec((None,) + shape, lambda s, j, pt: (s,) + (0,) * len(shape))
    kern = functools.partial(_mla_decode_kernel, n_pg=n_pg, page=page, n_t=n_t)
    grid_spec = pltpu.PrefetchScalarGridSpec(
        num_scalar_prefetch=1,
        grid=(n_seq, n_chunk),
        in_specs=([cpage_spec(i) for i in range(n_pg)] + [rpage_spec(i) for i in range(n_pg)]
                  + [seq_spec((rows, dc)), seq_spec((rows, dr)), seq_spec((n_t, dc)), seq_spec((dr, n_t))]),
        out_specs=seq_spec((rows, dc)),
        scratch_shapes=[pltpu.VMEM((n_pg * page, dc), BF16), pltpu.VMEM((dr, n_pg * page), BF16),
                        pltpu.VMEM((rows, 1), F32), pltpu.VMEM((rows, 1), F32),
                        pltpu.VMEM((rows, dc), F32)],
    )
    return pl.pallas_call(
        kern,
        out_shape=jax.ShapeDtypeStruct((n_seq, rows, dc), BF16),
        grid_spec=grid_spec,
        compiler_params=_cparams(("arbitrary", "arbitrary")),
        name="mla_decode",
    )(page_table, *([pool_c] * n_pg), *([pool_rt] * n_pg), q_lat, q_pe, c_new, rt_new)


def _mla_proj_kernel(x_ref, cos_ref, sin_ref, cost_ref, sint_ref, g_ref, w1_ref, qg_ref, kg_ref,
                     wqn_ref, wqp_ref, wqr_ref, wuk_ref, wuv_ref, *out_refs, sample, n_heads, qscale, lora):
    cosp = cos_ref[...]
    sinp = sin_ref[...]
    hn = _rms(x_ref[...], g_ref[...]).astype(BF16)
    a = _dot(hn, w1_ref[...])
    cqn = _rms(a[:, :lora], qg_ref[...]).astype(BF16)
    ckvn = _rms(a[:, lora:2 * lora], kg_ref[...])
    kp = a[:, 2 * lora:2 * lora + LANES] * cosp + a[:, 2 * lora + LANES:] * sinp
    rope_w = kp.shape[1] // 2
    ckvb = ckvn.astype(BF16)
    if sample:
        ckv_ref, kpe_ref, ql_ref, qpe_ref = out_refs
        qn = _dot(cqn, wqn_ref[...])
        qp = _dot(cqn, wqp_ref[...])
        qr = _dot(cqn, wqr_ref[...])
    else:
        ckv_ref, kpe_ref, qt_ref, kc_ref, vt_ref = out_refs
        cost = cost_ref[...]
        sint = sint_ref[...]
        kn = _dot(ckvb, wuk_ref[...])
        vt = _dot_nt(wuv_ref[...], ckvb).astype(BF16)
        tile = vt_ref.shape[2]
        for t in range(vt_ref.shape[0]):
            vt_ref[t] = vt[:, t * tile:(t + 1) * tile]
        qnt = _dot_nt(wqn_ref[...], cqn)
        qpt = _dot_nt(wqp_ref[...], cqn)
        qrt = _dot_nt(wqr_ref[...], cqn)
    ckv_ref[...] = ckvn
    kpe_ref[...] = kp[:, :rope_w]
    for h in range(n_heads):
        hs = slice(h * LANES, (h + 1) * LANES)
        if sample:
            q_rope = (qp[:, hs] * cosp + qr[:, hs] * sinp) * qscale
            ql_ref[h] = (_dot_nt(qn[:, hs].astype(BF16), wuk_ref[:, hs]) * qscale).astype(BF16)
            qpe_ref[h] = q_rope[:, :rope_w].astype(BF16)
        else:
            qt_ref[h, 0:LANES, :] = (qnt[hs] * qscale).astype(BF16)
            qt_ref[h, LANES:, :] = ((qpt[hs] * cost + qrt[hs] * sint) * qscale).astype(BF16)
            kc_ref[h, :, 0:LANES] = kn[:, hs].astype(BF16)
            kc_ref[h, :, LANES:] = kp.astype(BF16)


def _mla_proj(x, rope, rope_t, g, w1, qg, kg, wqn, wqp, wqr, wuk, wuv, *, sample, qscale, tile):
    m, d = x.shape
    lora = qg.shape[1]
    n_heads = wuk.shape[1] // LANES
    rope_w = LANES // 2
    tm = _row_tile(m, 512)
    tile = min(tile, tm)
    cosp, sinp = rope
    cost, sint = rope_t
    assert cosp.shape[0] % tm == 0
    n_pos = cosp.shape[0] // tm
    row = lambda width: pl.BlockSpec((tm, width), lambda i: (i, 0))
    pos_spec = pl.BlockSpec((tm, LANES), lambda i: (i % n_pos, 0))
    post_spec = pl.BlockSpec((LANES, tm), lambda i: (0, i % n_pos))
    if sample:
        hrow = lambda width: pl.BlockSpec((n_heads, tm, width), lambda i: (0, i, 0))
        out_shape = (jax.ShapeDtypeStruct((m, lora), F32), jax.ShapeDtypeStruct((m, rope_w), F32),
                     jax.ShapeDtypeStruct((n_heads, m, lora), BF16),
                     jax.ShapeDtypeStruct((n_heads, m, rope_w), BF16))
        out_specs = (row(lora), row(rope_w), hrow(lora), hrow(rope_w))
    else:
        out_shape = (jax.ShapeDtypeStruct((m, lora), F32), jax.ShapeDtypeStruct((m, rope_w), F32),
                     jax.ShapeDtypeStruct((n_heads, 2 * LANES, m), BF16),
                     jax.ShapeDtypeStruct((n_heads, m, 2 * LANES), BF16),
                     jax.ShapeDtypeStruct((m // tile, n_heads * LANES, tile), BF16))
        out_specs = (row(lora), row(rope_w),
                     pl.BlockSpec((n_heads, 2 * LANES, tm), lambda i: (0, 0, i)),
                     pl.BlockSpec((n_heads, tm, 2 * LANES), lambda i: (0, i, 0)),
                     pl.BlockSpec((tm // tile, n_heads * LANES, tile), lambda i: (i, 0, 0)))
    kern = functools.partial(_mla_proj_kernel, sample=sample, n_heads=n_heads, qscale=qscale, lora=lora)
    return pl.pallas_call(
        kern,
        out_shape=out_shape,
        grid=(m // tm,),
        in_specs=[row(d), pos_spec, pos_spec, post_spec, post_spec, _const_spec(g.shape),
                  _const_spec(w1.shape), _const_spec(qg.shape), _const_spec(kg.shape),
                  _const_spec(wqn.shape), _const_spec(wqp.shape), _const_spec(wqr.shape),
                  _const_spec(wuk.shape), _const_spec(wuv.shape)],
        out_specs=out_specs,
        compiler_params=_cparams(("arbitrary",)),
        name="mla_proj_sample" if sample else "mla_proj_prompt",
    )(x, cosp, sinp, cost, sint, g, w1, qg, kg, wqn, wqp, wqr, wuk, wuv)


def _head_matmul_kernel(o_ref, w_ref, y_ref):
    y_ref[...] = _dot(o_ref[...], w_ref[...]).astype(y_ref.dtype)


def _head_matmul(o, w):
    n_heads, m, c = o.shape
    v = w.shape[2]
    return pl.pallas_call(
        _head_matmul_kernel,
        out_shape=jax.ShapeDtypeStruct((m, n_heads * v), BF16),
        grid=(n_heads,),
        in_specs=[pl.BlockSpec((None, m, c), lambda h: (h, 0, 0)),
                  pl.BlockSpec((None, c, v), lambda h: (h, 0, 0))],
        out_specs=pl.BlockSpec((m, v), lambda h: (0, h)),
        compiler_params=_cparams(("arbitrary",)),
        name="mla_value_up",
    )(o, w)


def _proj_res_kernel(o_ref, x_ref, w_ref, g_ref, y_ref):
    y = _dot(o_ref[...].astype(BF16), w_ref[...])
    y_ref[...] = x_ref[...] + _rms(y, g_ref[...])


def _proj_res(o, x, w, g):
    m, d = x.shape
    k = o.shape[1]
    tm = _row_tile(m, 512)
    return pl.pallas_call(
        _proj_res_kernel,
        out_shape=jax.ShapeDtypeStruct((m, d), F32),
        grid=(m // tm,),
        in_specs=[pl.BlockSpec((tm, k), lambda i: (i, 0)), pl.BlockSpec((tm, d), lambda i: (i, 0)),
                  _const_spec(w.shape), _const_spec(g.shape)],
        out_specs=pl.BlockSpec((tm, d), lambda i: (i, 0)),
        compiler_params=_cparams(("arbitrary",)),
        name="attn_out_proj",
    )(o, x, w, g)


def _ffn_kernel(x_ref, g2_ref, g3_ref, win_ref, cw_ref, cb_ref, wout_ref, st_ref, y_ref, cs_ref,
                ext_ref, h_ref, *, tm, ff, shift, pad, chunk):
    i = pl.program_id(1)
    lo = pad - 2 * shift

    @pl.when(i == 0)
    def _load_state():
        ext_ref[lo:pad, :] = st_ref[...]

    x = x_ref[...]
    xn = _rms(x, g2_ref[...]).astype(BF16)
    c0 = math.sqrt(2.0 / math.pi)
    for c in range(ff // chunk):
        cs = slice(c * chunk, (c + 1) * chunk)
        gate = _dot(xn, win_ref[:, c * chunk:(c + 1) * chunk])
        val = _dot(xn, win_ref[:, ff + c * chunk:ff + (c + 1) * chunk])
        ext_ref[pad:pad + tm, cs] = gate
        conv = (cb_ref[:, cs] + ext_ref[lo:lo + tm, cs] * cw_ref[0:1, cs]
                + ext_ref[lo + shift:lo + shift + tm, cs] * cw_ref[1:2, cs] + gate * cw_ref[2:3, cs])
        act = 0.5 * conv * (1.0 + jnp.tanh(c0 * (conv + 0.044715 * (conv * conv * conv))))
        h_ref[:, cs] = (act * val).astype(BF16)
        tail = ext_ref[tm + lo:tm + pad, cs]
        cs_ref[:, cs] = tail
        ext_ref[lo:pad, cs] = tail
    y = _dot(h_ref[...], wout_ref[...])
    y_ref[...] = x + _rms(y, g3_ref[...])


def _conv_ffn(x, g2, g3, w_in, conv_w, conv_b, w_out, state, *, n_seq, shift, tm_cap):
    m, d = x.shape
    ff = w_out.shape[0]
    seq_len = m // n_seq
    tm = _row_tile(seq_len, tm_cap)
    nb = seq_len // tm
    pad = max(8, 2 * shift)
    chunk = 256
    assert ff % chunk == 0 and tm >= 2 * shift
    kern = functools.partial(_ffn_kernel, tm=tm, ff=ff, shift=shift, pad=pad, chunk=chunk)
    const = lambda shape: pl.BlockSpec(shape, lambda s, i: (0,) * len(shape), pipeline_mode=pl.Buffered(1))
    return pl.pallas_call(
        kern,
        out_shape=(jax.ShapeDtypeStruct((m, d), F32), jax.ShapeDtypeStruct((n_seq, 2 * shift, ff), F32)),
        grid=(n_seq, nb),
        in_specs=[pl.BlockSpec((tm, d), lambda s, i: (s * nb + i, 0)), const(g2.shape), const(g3.shape),
                  const(w_in.shape), const(conv_w.shape), const(conv_b.shape), const(w_out.shape),
                  pl.BlockSpec((None, 2 * shift, ff), lambda s, i: (s, 0, 0))],
        out_specs=(pl.BlockSpec((tm, d), lambda s, i: (s * nb + i, 0)),
                   pl.BlockSpec((None, 2 * shift, ff), lambda s, i: (s, 0, 0))),
        scratch_shapes=[pltpu.VMEM((pad + tm, ff), F32), pltpu.VMEM((tm, ff), BF16)],
        compiler_params=_cparams(("arbitrary", "arbitrary")),
        name="conv_ffn",
    )(x, g2, g3, w_in, conv_w, conv_b, w_out, state)


def _rope_tables(pos):
    half = 32
    inv = jnp.power(ROPE_THETA, -jnp.arange(half, dtype=F32) / half)
    ang = pos.astype(F32)[:, None] * inv
    z = jnp.zeros((pos.shape[0], LANES - 2 * half), F32)
    cos, sin = jnp.cos(ang), jnp.sin(ang)
    return jnp.concatenate([cos, cos, z], axis=1), jnp.concatenate([sin, sin, z], axis=1)


def _rot_cols(w):
    half = w.shape[-1] // 2
    return jnp.concatenate([-w[..., half:], w[..., :half]], axis=-1)


def _pad_axis(a, size, axis):
    shape = list(a.shape)
    shape[axis] = size - a.shape[axis]
    return jnp.concatenate([a, jnp.zeros(shape, a.dtype)], axis=axis)


def kernel(x_prompt, x_sample, cache_k_a, cache_v_a, cache_ckv_b, cache_kpe_b, state_conv, page_table,
           meta_tokens, rel_bias, norm_g, w_qkv_a, lambda_q1, lambda_k1, lambda_q2, lambda_k2, subln_g, w_o_a,
           w_dqkv_b, q_norm_g, w_uq, kv_norm_g, w_uk, w_uv, w_o_b, w_ffn_in, conv_w, conv_b, w_ffn_out):
    n_b, seq, d = x_prompt.shape
    n_s, n_t, _ = x_sample.shape
    n_meta = meta_tokens.shape[0]
    depth = norm_g.shape[0]
    ff = w_ffn_out.shape[1]
    a_heads, a_dh = cache_k_a.shape[3], cache_k_a.shape[5]
    a_w = a_heads * 2 * a_dh
    n_pool, page = cache_k_a.shape[1], cache_k_a.shape[2]
    n_pages = page_table.shape[1]
    past = n_pages * page
    b_heads, b_nope = w_uk.shape[2], w_uk.shape[3]
    lora_q, lora_kv, b_rope = q_norm_g.shape[1], kv_norm_g.shape[1], cache_kpe_b.shape[3]
    mla_scale = (b_nope + b_rope) ** -0.5
    T = ATTN_TILE
    assert seq % (2 * T) == 0 and n_meta <= LANES and n_t <= 8 and b_nope == LANES and 2 * a_dh == LANES
    assert lora_q == lora_kv and b_rope == LANES // 2 and page == LANES

    xm = meta_tokens.astype(F32)
    xp = x_prompt.reshape(n_b * seq, d)
    xs = x_sample.transpose(1, 0, 2).reshape(n_t * n_s, d)

    pos_m = jnp.arange(n_meta, dtype=jnp.int32)
    pos_p = n_meta + jnp.arange(seq, dtype=jnp.int32)
    pos_s = jnp.repeat(past + jnp.arange(n_t, dtype=jnp.int32), n_s)
    rope_m, rope_p, rope_s = _rope_tables(pos_m), _rope_tables(pos_p), _rope_tables(pos_s)
    tabs_t = lambda tabs: tuple(t.T for t in tabs)

    def to_seq_major(a):
        return a.reshape(n_t, n_s, a.shape[-1]).transpose(1, 0, 2)

    def with_meta(meta_rows, main_rows):
        w = meta_rows.shape[-1]
        return jnp.concatenate([jnp.broadcast_to(meta_rows[None], (n_b, n_meta, w)),
                                main_rows.reshape(n_b, seq, w)], axis=1)

    outs = {k: [] for k in ("k_a_p", "v_a_p", "k_a_s", "v_a_s", "ckv_p", "kpe_p", "ckv_s", "kpe_s",
                            "conv_p", "conv_s")}
    for i in range(depth):
        g = norm_g[i].astype(F32)
        g0, g1, g2, g3 = (g[r:r + 1] for r in range(4))
        j = i // 2
        if i % 2 == 0:
            lam_init = 0.8 - 0.6 * math.exp(-0.3 * i)
            lam_vecs = jnp.stack([lambda_q1[j], lambda_k1[j], lambda_q2[j], lambda_k2[j]]).astype(F32)
            sg = subln_g[j].astype(F32)[None]
            w_qkv = w_qkv_a[j].astype(BF16)
            w_qv_t = jnp.concatenate([w_qkv[:, :a_w].T, w_qkv[:, 2 * a_w:].T], axis=0)
            qscale = a_dh ** -0.5 * LOG2E
            b31 = rel_bias[N_BUCKETS - 1].astype(F32) * LOG2E
            proj = functools.partial(_qkv_proj, g=g0, w=w_qkv, wt=w_qv_t, qscale=qscale, q_tile=2 * T,
                                     v_tile=T)
            km, vm, kmb, qmt, vmt = proj(xm, transposed=True)
            kp, vp, kpb, qpt, vpt = proj(xp, transposed=True)
            ks, vs, qs = proj(xs, transposed=False)
            kmb = _pad_axis(kmb, LANES, 0)
            qmt, vmt = _pad_axis(qmt, LANES, 2), _pad_axis(vmt[0], LANES, 1)
            bias_meta = _bias_tiles(rel_bias, LANES, ((0, LANES),), True)
            bias_main = _bias_tiles(rel_bias, T, ((-T, 2 * T), (0, 2 * T), (T, 2 * T), (-n_meta, 2 * T)), True)
            flash = functools.partial(_flash_diff, km=kmb, vmt=vmt, b31=b31, lam_vecs=lam_vecs,
                                      sg_col=sg.T, n_heads=a_heads, n_meta=n_meta, lam_init=lam_init)
            om = flash(qmt, kmb, vmt[None], bias=bias_meta, n_seq=1, has_main=False)[:n_meta]
            op = flash(qpt, kpb, vpt, bias=bias_main, n_seq=n_b, has_main=True)
            n_pg = min(8, n_pages)
            n_pair = a_heads // 2
            bias_dec = _bias_tiles(rel_bias, 8, ((-n_pg * page, n_pg * page), (0, LANES)), False)[:, :n_t]
            expand = lambda b: jnp.broadcast_to(b[:, None], (a_heads, 2, n_t, b.shape[-1])).reshape(
                a_heads * 2 * n_t, b.shape[-1])
            b_last = expand(bias_dec[:, :, :n_pg * page])
            col = jnp.arange(LANES)[None, None, :]
            b_new = expand(jnp.where((col < n_t) & (col <= jnp.arange(n_t)[None, :, None]),
                                     bias_dec[:, :, n_pg * page:], NEG_INF))
            b_far = jnp.repeat(b31, 2 * n_t)[:, None]
            q5 = to_seq_major(qs).reshape(n_s, n_t, n_pair, 4, a_dh)
            eye = jnp.eye(4, dtype=BF16)
            qbd = (q5.transpose(0, 2, 3, 1, 4)[:, :, :, :, None, :]
                   * eye[None, None, :, None, :, None]).reshape(n_s, n_pair, 4 * n_t, 4 * a_dh)
            pool_kt = cache_k_a[j].transpose(0, 2, 3, 4, 1).reshape(n_pool, a_w, page)
            pool_v = cache_v_a[j].reshape(n_pool, page * a_heads, 2 * a_dh)
            os_ = _diff_decode(page_table, pool_kt, pool_v, qbd, to_seq_major(ks).transpose(0, 2, 1),
                               to_seq_major(vs), b_far, b_last, b_new, lam_vecs, sg, n_pg=n_pg,
                               n_heads=a_heads, lam_init=lam_init)
            os_ = os_.transpose(1, 0, 2).reshape(n_t * n_s, a_w)
            w_o = w_o_a[j].astype(BF16)
            outs["k_a_p"].append(with_meta(km, kp).reshape(n_b, n_meta + seq, a_heads, 2, a_dh))
            outs["v_a_p"].append(with_meta(vm, vp).reshape(n_b, n_meta + seq, a_heads, 2 * a_dh))
            outs["k_a_s"].append(to_seq_major(ks).reshape(n_s, n_t, a_heads, 2, a_dh))
            outs["v_a_s"].append(to_seq_major(vs).reshape(n_s, n_t, a_heads, 2 * a_dh))
        else:
            wd = w_dqkv_b[j]
            w_kpe = wd[:, lora_q + lora_kv:]
            w1 = jnp.concatenate([wd[:, :lora_q + lora_kv], _pad_axis(w_kpe, LANES, 1),
                                  _pad_axis(_rot_cols(w_kpe), LANES, 1)], axis=1).astype(BF16)
            wq = w_uq[j].reshape(lora_q, b_heads, b_nope + b_rope)
            wqn = wq[:, :, :b_nope].reshape(lora_q, b_heads * b_nope).astype(BF16)
            wqp = _pad_axis(wq[:, :, b_nope:], LANES, 2).reshape(lora_q, b_heads * LANES).astype(BF16)
            wqr = _pad_axis(_rot_cols(wq[:, :, b_nope:]), LANES, 2).reshape(
                lora_q, b_heads * LANES).astype(BF16)
            wuk = w_uk[j].reshape(lora_kv, b_heads * b_nope).astype(BF16)
            wuv = w_uv[j].reshape(lora_kv, -1).astype(BF16)
            qg, kg = q_norm_g[j].astype(F32)[None], kv_norm_g[j].astype(F32)[None]
            qscale = mla_scale * LOG2E
            proj = functools.partial(_mla_proj, g=g0, w1=w1, qg=qg, kg=kg, wuk=wuk, qscale=qscale,
                                     tile=2 * T)
            proj_p = functools.partial(proj, wqn=wqn.T, wqp=wqp.T, wqr=wqr.T, wuv=wuv.T, sample=False)
            ckm, kpm, qmt, kcm, vmt = proj_p(xm, rope_m, tabs_t(rope_m))
            ckp, kpp, qpt, kcp, vpt = proj_p(xp, rope_p, tabs_t(rope_p))
            cks, kps, qls, qps = proj(xs, rope_s, tabs_t(rope_s), wqn=wqn, wqp=wqp, wqr=wqr, wuv=wuv,
                                      sample=True)
            kcm = _pad_axis(kcm, LANES, 1)
            qmt, vmt = _pad_axis(qmt, LANES, 2), _pad_axis(vmt[0], LANES, 1)
            om = _flash_mla(qmt, kcm, vmt[None], kcm, vmt, n_seq=1, has_main=False, n_meta=n_meta)[:n_meta]
            op = _flash_mla(qpt, kcp, vpt, kcm, vmt, n_seq=n_b, has_main=True, n_meta=n_meta)
            n_pg = n_pages
            hseq = lambda a: a.reshape(b_heads, n_t, n_s, a.shape[-1]).transpose(2, 0, 1, 3).reshape(
                n_s, b_heads * n_t, a.shape[-1])
            pool_rt = cache_kpe_b[j].transpose(0, 2, 1)
            ol = _mla_decode(page_table, cache_ckv_b[j], pool_rt, hseq(qls), hseq(qps),
                             to_seq_major(cks), to_seq_major(kps).transpose(0, 2, 1), n_pg=n_pg)
            ol = ol.reshape(n_s, b_heads, n_t, lora_kv).transpose(1, 2, 0, 3).reshape(
                b_heads, n_t * n_s, lora_kv)
            os_ = _head_matmul(ol, w_uv[j].transpose(1, 0, 2).astype(BF16))
            w_o = w_o_b[j].astype(BF16)
            outs["ckv_p"].append(with_meta(ckm, ckp))
            outs["kpe_p"].append(with_meta(kpm, kpp))
            outs["ckv_s"].append(to_seq_major(cks))
            outs["kpe_s"].append(to_seq_major(kps))
        xm = _proj_res(om, xm, w_o, g1)
        xp = _proj_res(op, xp, w_o, g1)
        xs = _proj_res(os_, xs, w_o, g1)

        w_in, w_out = w_ffn_in[i].astype(BF16), w_ffn_out[i].astype(BF16)
        cw, cb = conv_w[i].astype(F32), conv_b[i].astype(F32)[None]
        ffn = functools.partial(_conv_ffn, g2=g2, g3=g3, w_in=w_in, conv_w=cw, conv_b=cb, w_out=w_out)
        xm, cm = ffn(xm, state=jnp.zeros((1, 2, ff), F32), n_seq=1, shift=1, tm_cap=512)
        xp, cp = ffn(xp, state=jnp.broadcast_to(cm, (n_b, 2, ff)), n_seq=n_b, shift=1, tm_cap=512)
        st = state_conv[i].astype(F32).transpose(1, 0, 2).reshape(1, 2 * n_s, ff)
        xs, cs = ffn(xs, state=st, n_seq=1, shift=n_s, tm_cap=2 * n_s)
        outs["conv_p"].append(cp)
        outs["conv_s"].append(cs.reshape(2, n_s, ff).transpose(1, 0, 2))

    y_prompt = xp.reshape(n_b, seq, d)
    y_sample = to_seq_major(xs)
    return (y_prompt, y_sample,
            jnp.stack(outs["k_a_p"]), jnp.stack(outs["v_a_p"]), jnp.stack(outs["k_a_s"]),
            jnp.stack(outs["v_a_s"]), jnp.stack(outs["ckv_p"]), jnp.stack(outs["kpe_p"]),
            jnp.stack(outs["ckv_s"]), jnp.stack(outs["kpe_s"]), jnp.stack(outs["conv_p"]),
            jnp.stack(outs["conv_s"]))
```

```python
import functools
import math

import jax
import jax.numpy as jnp
from jax import lax
from jax.experimental import pallas as pl
from jax.experimental.pallas import tpu as pltpu

F32 = jnp.float32
BF16 = jnp.bfloat16

EPS = 1e-6
NEG_INF = -1e30
LOG2E = 1.4426950408889634
N_BUCKETS = 32
MAX_DISTANCE = 128
ROPE_THETA = 10000.0
LANES = 128
ATTN_TILE = 256
VMEM_LIMIT = 56 * 1024 * 1024


def _cparams(sem):
    return pltpu.CompilerParams(dimension_semantics=sem, vmem_limit_bytes=VMEM_LIMIT)


def _const_spec(shape):
    nd = len(shape)
    return pl.BlockSpec(shape, lambda *_: (0,) * nd, pipeline_mode=pl.Buffered(1))


def _rms(x, g):
    return x * lax.rsqrt(jnp.mean(x * x, axis=-1, keepdims=True) + EPS) * g


def _dot(a, b):
    return jnp.dot(a, b, preferred_element_type=F32)


def _dot_nt(a, b):
    return lax.dot_general(a, b, (((1,), (1,)), ((), ())), preferred_element_type=F32)


def _row_tile(m, cap):
    t = min(m, cap)
    while m % t:
        t //= 2
    return t


def _lambda(lam_ref, lam_init):
    lv = lam_ref[...]
    return (jnp.exp(jnp.sum(lv[0:1] * lv[1:2], axis=-1, keepdims=True))
            - jnp.exp(jnp.sum(lv[2:3] * lv[3:4], axis=-1, keepdims=True)) + lam_init)


def _bias_kernel(tbl_ref, o_ref, *, segs, keys_on_rows):
    h = pl.program_id(0)
    rows = o_ref.shape[0]
    col0 = 0
    max_exact = N_BUCKETS // 2
    for off, width in segs:
        r = lax.broadcasted_iota(jnp.int32, (rows, width), 0)
        c = lax.broadcasted_iota(jnp.int32, (rows, width), 1)
        rel = (r - c + off) if keys_on_rows else (c - r + off)
        n = jnp.maximum(-rel, 0)
        nf = jnp.maximum(n, 1).astype(F32)
        log_b = max_exact + (jnp.log(nf / max_exact) / math.log(MAX_DISTANCE / max_exact)
                             * (N_BUCKETS - max_exact)).astype(jnp.int32)
        bucket = jnp.where(n < max_exact, n, jnp.minimum(log_b, N_BUCKETS - 1))
        val = jnp.full((rows, width), tbl_ref[N_BUCKETS - 1, h], F32)
        for b in range(N_BUCKETS - 1):
            val = jnp.where(bucket == b, tbl_ref[b, h], val)
        o_ref[:, col0:col0 + width] = val * LOG2E
        col0 += width


def _bias_tiles(rel_bias, rows, segs, keys_on_rows):
    n_heads = rel_bias.shape[1]
    width = sum(w for _, w in segs)
    return pl.pallas_call(
        functools.partial(_bias_kernel, segs=segs, keys_on_rows=keys_on_rows),
        out_shape=jax.ShapeDtypeStruct((n_heads, rows, width), F32),
        grid=(n_heads,),
        in_specs=[pl.BlockSpec(memory_space=pltpu.SMEM)],
        out_specs=pl.BlockSpec((None, rows, width), lambda h: (h, 0, 0)),
        compiler_params=_cparams(("arbitrary",)),
        name="rel_bias_tiles",
    )(rel_bias)


def _qkv_kernel(x_ref, g_ref, w_ref, wt_ref, *out_refs, qscale, transposed):
    xn = _rms(x_ref[...], g_ref[...]).astype(BF16)
    if transposed:
        k_ref, v_ref, kb_ref, qt_ref, vt_ref = out_refs
    else:
        k_ref, v_ref, q_ref = out_refs
    n = k_ref.shape[1]
    k = _dot(xn, w_ref[:, n:2 * n])
    k_ref[...] = k
    v = _dot(xn, w_ref[:, 2 * n:])
    v_ref[...] = v
    if transposed:
        kb_ref[...] = k.astype(BF16)
        qt = (_dot_nt(wt_ref[0:n, :], xn) * qscale).astype(BF16)
        vt = _dot_nt(wt_ref[n:2 * n, :], xn).astype(BF16)
        for ref, val in ((qt_ref, qt), (vt_ref, vt)):
            tile = ref.shape[2]
            for t in range(ref.shape[0]):
                ref[t] = val[:, t * tile:(t + 1) * tile]
    else:
        q_ref[...] = (_dot(xn, w_ref[:, :n]) * qscale).astype(BF16)


def _qkv_proj(x, g, w, wt, qscale, *, transposed, q_tile, v_tile):
    m, d = x.shape
    n = w.shape[1] // 3
    tm = _row_tile(m, 512)
    row = lambda width: pl.BlockSpec((tm, width), lambda i: (i, 0))

    def t_out(tile):
        tile = min(tile, tm)
        return (jax.ShapeDtypeStruct((m // tile, n, tile), BF16),
                pl.BlockSpec((tm // tile, n, tile), lambda i: (i, 0, 0)))

    if transposed:
        (q_shape, q_spec), (v_shape, v_spec) = t_out(q_tile), t_out(v_tile)
        out_shape = (jax.ShapeDtypeStruct((m, n), F32), jax.ShapeDtypeStruct((m, n), F32),
                     jax.ShapeDtypeStruct((m, n), BF16), q_shape, v_shape)
        out_specs = (row(n), row(n), row(n), q_spec, v_spec)
    else:
        out_shape = (jax.ShapeDtypeStruct((m, n), F32), jax.ShapeDtypeStruct((m, n), F32),
                     jax.ShapeDtypeStruct((m, n), BF16))
        out_specs = (row(n), row(n), row(n))
    return pl.pallas_call(
        functools.partial(_qkv_kernel, qscale=qscale, transposed=transposed),
        out_shape=out_shape,
        grid=(m // tm,),
        in_specs=[row(d), _const_spec((1, d)), _const_spec(w.shape), _const_spec(wt.shape)],
        out_specs=out_specs,
        compiler_params=_cparams(("arbitrary",)),
        name="qkv_proj_t" if transposed else "qkv_proj",
    )(x, g, w, wt)


def _flash_kernel(*refs, diff, has_main, tq, n_meta, lam_init):
    if diff:
        (b31_ref, lam_ref, sg_ref, qt_ref, k_ref, vt_ref, km_ref, vmt_ref, bias_ref,
         o_ref, acc_ref, q2_ref) = refs
    else:
        qt_ref, k_ref, vt_ref, km_ref, vmt_ref, o_ref, acc_ref = refs
    h = pl.program_id(1)
    qi = pl.program_id(2)
    T = ATTN_TILE
    nq = acc_ref.shape[1]

    if diff:
        qt = qt_ref[...]
        row = lax.broadcasted_iota(jnp.int32, qt.shape, 0)
        half = qt.shape[0] // 2
        q2_ref[:, :tq] = jnp.where(row < half, qt, jnp.zeros_like(qt))
        q2_ref[:, tq:] = jnp.where(row >= half, qt, jnp.zeros_like(qt))
        qq = q2_ref[...]
        b31 = b31_ref[h]
    else:
        qq = qt_ref[...]
        b31 = None
    both = (lambda a: jnp.concatenate([a, a], axis=1)) if diff else (lambda a: a)
    acc_ref[...] = jnp.zeros(acc_ref.shape, F32)

    def scores(kt, bias, mask):
        s = _dot(kt, qq)
        if bias is not None:
            s = s + bias
        if mask is not None:
            s = jnp.where(mask, s, NEG_INF)
        return s

    def absorb(s, vtt, carry, shift=None):
        m_prev, l_prev = carry
        if shift is not None:
            m_prev = m_prev - shift
        m_new = jnp.maximum(m_prev, jnp.max(s, axis=0, keepdims=True))
        alpha = jnp.exp2(m_prev - m_new)
        p = jnp.exp2(s - m_new)
        l_new = alpha * l_prev + jnp.sum(p, axis=0, keepdims=True)
        acc_ref[...] = alpha * acc_ref[...] + _dot(vtt, p.astype(BF16))
        if shift is not None:
            m_new = m_new + shift
        return m_new, l_new

    carry = (jnp.full((1, nq), NEG_INF, F32), jnp.zeros((1, nq), F32))

    r = lax.broadcasted_iota(jnp.int32, (LANES, tq), 0)
    c = lax.broadcasted_iota(jnp.int32, (LANES, tq), 1)
    if has_main:
        meta_mask = r < n_meta
        meta_bias = both(jnp.where(qi == 0, bias_ref[0:LANES, 3 * tq:4 * tq], b31)) if diff else None
    else:
        meta_mask = (r < n_meta) & (r <= c)
        meta_bias = both(bias_ref[...]) if diff else None
    carry = absorb(scores(km_ref[...], meta_bias, both(meta_mask)), vmt_ref[...], carry)

    if has_main:
        assert tq == 2 * T
        fsub = vt_ref.shape[2] // T

        def pair(j, bias_a, mask_a, bias_b, mask_b, carry, shift_a=None, shift_b=None):
            off = pl.multiple_of(j * 2 * T, 2 * T)
            if fsub == 1:
                va, vb = vt_ref[2 * j], vt_ref[2 * j + 1]
            else:
                va, vb = vt_ref[j, :, 0:T], vt_ref[j, :, T:2 * T]
            sa = scores(k_ref[pl.ds(off, T), :], bias_a, mask_a)
            sb = scores(k_ref[pl.ds(pl.multiple_of(off + T, T), T), :], bias_b, mask_b)
            return absorb(sb, vb, absorb(sa, va, carry, shift_a), shift_b)

        n_far = qi - 1 if diff else qi
        carry = lax.fori_loop(0, jnp.maximum(n_far, 0),
                              lambda j, carry: pair(j, None, None, None, None, carry, b31, b31), carry)
        if diff:
            live = jnp.broadcast_to(qi >= 1, (T, nq))
            carry = pair(jnp.maximum(qi - 1, 0), None, live, both(bias_ref[:, 0:tq]), live, carry, b31)
        rr = lax.broadcasted_iota(jnp.int32, (T, tq), 0)
        cc = lax.broadcasted_iota(jnp.int32, (T, tq), 1)
        bias_a = both(bias_ref[:, tq:2 * tq]) if diff else None
        bias_b = both(bias_ref[:, 2 * tq:3 * tq]) if diff else None
        carry = pair(qi, bias_a, both(rr <= cc), bias_b, both(rr + T <= cc), carry)

    _, l_fin = carry
    a = acc_ref[...] / l_fin
    if diff:
        o = a[:, :tq] - _lambda(lam_ref, lam_init) * a[:, tq:]
        o = o * lax.rsqrt(jnp.mean(o * o, axis=0, keepdims=True) + EPS) * sg_ref[...] * (1.0 - lam_init)
    else:
        o = a
    o_ref[...] = o.T.astype(o_ref.dtype)


def _flash_diff(qt, kb, vt, km, vmt, bias, b31, lam_vecs, sg_col, *, n_seq, n_heads, has_main, n_meta,
                lam_init):
    hd = kb.shape[1]
    dv = hd // n_heads
    tq, tkv = qt.shape[2], vt.shape[2]
    seq_len = qt.shape[0] * tq // n_seq
    nq = seq_len // tq
    kern = functools.partial(_flash_kernel, diff=True, has_main=has_main, tq=tq, n_meta=n_meta,
                             lam_init=lam_init)
    if has_main:
        kspec = pl.BlockSpec((seq_len, dv), lambda b, h, i: (b, h))
        vspec = pl.BlockSpec((seq_len // tkv, dv, tkv), lambda b, h, i: (b, h, 0))
    else:
        kspec = pl.BlockSpec((LANES, dv), lambda b, h, i: (0, h))
        vspec = pl.BlockSpec((1, dv, LANES), lambda b, h, i: (0, h, 0))
    return pl.pallas_call(
        kern,
        out_shape=jax.ShapeDtypeStruct((n_seq * seq_len, hd), BF16),
        grid=(n_seq, n_heads, nq),
        in_specs=[pl.BlockSpec(memory_space=pltpu.SMEM), _const_spec(lam_vecs.shape),
                  _const_spec(sg_col.shape),
                  pl.BlockSpec((None, dv, tq), lambda b, h, i: (b * nq + i, h, 0)),
                  kspec, vspec,
                  pl.BlockSpec((LANES, dv), lambda b, h, i: (0, h)),
                  pl.BlockSpec((dv, LANES), lambda b, h, i: (h, 0)),
                  pl.BlockSpec((None,) + bias.shape[1:], lambda b, h, i: (h, 0, 0))],
        out_specs=pl.BlockSpec((tq, dv), lambda b, h, i: (b * nq + i, h)),
        scratch_shapes=[pltpu.VMEM((dv, 2 * tq), F32), pltpu.VMEM((dv, 2 * tq), BF16)],
        compiler_params=_cparams(("arbitrary", "arbitrary", "arbitrary")),
        name="diff_flash" if has_main else "diff_meta_attn",
    )(b31, lam_vecs, sg_col, qt, kb, vt, km, vmt, bias)


def _flash_mla(qt, kc, vt, km, vmt, *, n_seq, has_main, n_meta):
    T = ATTN_TILE
    n_heads, dq, m = qt.shape
    dv = vmt.shape[0] // n_heads
    seq_len = m // n_seq
    tkv = vt.shape[2]
    tq = 2 * T if has_main else seq_len
    nq = seq_len // tq
    kern = functools.partial(_flash_kernel, diff=False, has_main=has_main, tq=tq, n_meta=n_meta,
                             lam_init=0.0)
    if has_main:
        kspec = pl.BlockSpec((None, seq_len, dq), lambda b, h, i: (h, b, 0))
        vspec = pl.BlockSpec((seq_len // tkv, dv, tkv), lambda b, h, i: (b, h, 0))
    else:
        kspec = pl.BlockSpec((None, LANES, dq), lambda b, h, i: (h, 0, 0))
        vspec = pl.BlockSpec((1, dv, LANES), lambda b, h, i: (0, h, 0))
    return pl.pallas_call(
        kern,
        out_shape=jax.ShapeDtypeStruct((m, n_heads * dv), BF16),
        grid=(n_seq, n_heads, nq),
        in_specs=[pl.BlockSpec((None, dq, tq), lambda b, h, i: (h, 0, b * nq + i)),
                  kspec, vspec,
                  pl.BlockSpec((None, LANES, dq), lambda b, h, i: (h, 0, 0)),
                  pl.BlockSpec((dv, LANES), lambda b, h, i: (h, 0))],
        out_specs=pl.BlockSpec((tq, dv), lambda b, h, i: (b * nq + i, h)),
        scratch_shapes=[pltpu.VMEM((dv, tq), F32)],
        compiler_params=_cparams(("arbitrary", "arbitrary", "arbitrary")),
        name="mla_flash" if has_main else "mla_meta_attn",
    )(qt, kc, vt, km, vmt)


def _online_update(s, m_ref, l_ref):
    m_prev = m_ref[...]
    m_new = jnp.maximum(m_prev, jnp.max(s, axis=-1, keepdims=True))
    alpha = jnp.exp2(m_prev - m_new)
    p = jnp.exp2(s - m_new)
    l_ref[...] = alpha * l_ref[...] + jnp.sum(p, axis=-1, keepdims=True)
    m_ref[...] = m_new
    return alpha, p.astype(BF16)


def _diff_decode_kernel(pt_ref, *refs, n_pg, page, n_pair, n_heads, lam_init):
    del pt_ref
    k_pages = refs[:n_pg]
    v_pages = refs[n_pg:2 * n_pg]
    (q_ref, kn_ref, vn_ref, bfar_ref, blast_ref, bnew_ref, lam_ref, sg_ref,
     o_ref, kc_ref, vc_ref, m_ref, l_ref, acc_ref) = refs[2 * n_pg:]
    j = pl.program_id(1)
    last = pl.num_programs(1) - 1
    rp = q_ref.shape[1]
    wp = q_ref.shape[2]
    n_t = kn_ref.shape[1]
    dv = wp // 2

    @pl.when(j == 0)
    def _init():
        m_ref[...] = jnp.full(m_ref.shape, NEG_INF, F32)
        l_ref[...] = jnp.zeros(l_ref.shape, F32)
        acc_ref[...] = jnp.zeros(acc_ref.shape, F32)

    def attend(n_keys, bias):
        s = jnp.concatenate(
            [_dot(q_ref[pr], kc_ref[pr * wp:(pr + 1) * wp, 0:n_keys]) for pr in range(n_pair)], axis=0)
        alpha, p = _online_update(s + bias, m_ref, l_ref)
        for pr in range(n_pair):
            rows = slice(pr * rp, (pr + 1) * rp)
            acc_ref[pr] = alpha[rows] * acc_ref[pr] + _dot(p[rows], vc_ref[0:n_keys, pr * wp:(pr + 1) * wp])

    for i in range(n_pg):
        kc_ref[:, i * page:(i + 1) * page] = k_pages[i][...].astype(BF16)
        for hh in range(n_heads):
            vc_ref[i * page:(i + 1) * page, hh * dv:(hh + 1) * dv] = (
                v_pages[i][pl.ds(hh, page, stride=n_heads), :].astype(BF16))
    attend(n_pg * page, jnp.where(j == last, blast_ref[...], bfar_ref[...]))

    @pl.when(j == last)
    def _finish():
        kc_ref[:, 0:LANES] = jnp.zeros((kc_ref.shape[0], LANES), BF16)
        vc_ref[0:LANES, :] = jnp.zeros((LANES, vc_ref.shape[1]), BF16)
        kc_ref[:, 0:n_t] = kn_ref[...].astype(BF16)
        vc_ref[0:n_t, :] = vn_ref[...].astype(BF16)
        attend(LANES, bnew_ref[...])
        lam = _lambda(lam_ref, lam_init)
        inv_l = 1.0 / l_ref[...]
        for pr in range(n_pair):
            a = acc_ref[pr] * inv_l[pr * rp:(pr + 1) * rp]
            for hl in range(2):
                r0 = hl * 2 * n_t
                o = (a[r0:r0 + n_t, hl * dv:(hl + 1) * dv]
                     - lam * a[r0 + n_t:r0 + 2 * n_t, hl * dv:(hl + 1) * dv])
                o = _rms(o, sg_ref[...]) * (1.0 - lam_init)
                hcol = (2 * pr + hl) * dv
                o_ref[:, hcol:hcol + dv] = o


def _diff_decode(page_table, pool_kt, pool_v, qbd, kt_new, v_new, b_far, b_last, b_new, lam_vecs, subln_g,
                 *, n_pg, n_heads, lam_init):
    n_seq, n_pages = page_table.shape
    _, width, page = pool_kt.shape
    dv = pool_v.shape[2]
    n_pair, rp, wp = qbd.shape[1:]
    n_t = v_new.shape[1]
    n_chunk = n_pages // n_pg
    n_rows = n_pair * rp

    def kpage_spec(i):
        return pl.BlockSpec((None, width, page), lambda s, j, pt: (pt[s, j * n_pg + i], 0, 0))

    def vpage_spec(i):
        return pl.BlockSpec((None, page * n_heads, dv), lambda s, j, pt: (pt[s, j * n_pg + i], 0, 0))

    seq_spec = lambda shape: pl.BlockSpec((None,) + shape, lambda s, j, pt: (s,) + (0,) * len(shape))
    const = lambda shape: pl.BlockSpec(shape, lambda s, j, pt: (0,) * len(shape))
    kern = functools.partial(_diff_decode_kernel, n_pg=n_pg, page=page, n_pair=n_pair, n_heads=n_heads,
                             lam_init=lam_init)
    grid_spec = pltpu.PrefetchScalarGridSpec(
        num_scalar_prefetch=1,
        grid=(n_seq, n_chunk),
        in_specs=([kpage_spec(i) for i in range(n_pg)] + [vpage_spec(i) for i in range(n_pg)]
                  + [seq_spec((n_pair, rp, wp)), seq_spec((width, n_t)), seq_spec((n_t, width)),
                     const(b_far.shape), const(b_last.shape), const(b_new.shape),
                     const(lam_vecs.shape), const(subln_g.shape)]),
        out_specs=seq_spec((n_t, width)),
        scratch_shapes=[pltpu.VMEM((width, n_pg * page), BF16), pltpu.VMEM((n_pg * page, width), BF16),
                        pltpu.VMEM((n_rows, 1), F32), pltpu.VMEM((n_rows, 1), F32),
                        pltpu.VMEM((n_pair, rp, wp), F32)],
    )
    return pl.pallas_call(
        kern,
        out_shape=jax.ShapeDtypeStruct((n_seq, n_t, width), F32),
        grid_spec=grid_spec,
        compiler_params=_cparams(("arbitrary", "arbitrary")),
        name="diff_decode",
    )(page_table, *([pool_kt] * n_pg), *([pool_v] * n_pg), qbd, kt_new, v_new, b_far, b_last, b_new,
      lam_vecs, subln_g)


def _mla_decode_kernel(pt_ref, *refs, n_pg, page, n_t):
    del pt_ref
    c_pages = refs[:n_pg]
    r_pages = refs[n_pg:2 * n_pg]
    (ql_ref, qp_ref, cn_ref, rn_ref, o_ref, cc_ref, rc_ref, m_ref, l_ref, acc_ref) = refs[2 * n_pg:]
    j = pl.program_id(1)
    last = pl.num_programs(1) - 1

    @pl.when(j == 0)
    def _init():
        m_ref[...] = jnp.full(m_ref.shape, NEG_INF, F32)
        l_ref[...] = jnp.zeros(l_ref.shape, F32)
        acc_ref[...] = jnp.zeros(acc_ref.shape, F32)

    def attend(n_keys, mask):
        ckv = cc_ref[0:n_keys, :]
        s = _dot_nt(ql_ref[...], ckv) + _dot(qp_ref[...], rc_ref[:, 0:n_keys])
        if mask is not None:
            s = jnp.where(mask, s, NEG_INF)
        alpha, p = _online_update(s, m_ref, l_ref)
        acc_ref[...] = alpha * acc_ref[...] + _dot(p, ckv)

    for i in range(n_pg):
        cc_ref[i * page:(i + 1) * page, :] = c_pages[i][...].astype(BF16)
        rc_ref[:, i * page:(i + 1) * page] = r_pages[i][...].astype(BF16)
    attend(n_pg * page, None)

    @pl.when(j == last)
    def _finish():
        cc_ref[0:LANES, :] = jnp.zeros((LANES, cc_ref.shape[1]), BF16)
        rc_ref[:, 0:LANES] = jnp.zeros((rc_ref.shape[0], LANES), BF16)
        cc_ref[0:n_t, :] = cn_ref[...].astype(BF16)
        rc_ref[:, 0:n_t] = rn_ref[...].astype(BF16)
        rows = ql_ref.shape[0]
        r = lax.broadcasted_iota(jnp.int32, (rows, LANES), 0)
        c = lax.broadcasted_iota(jnp.int32, (rows, LANES), 1)
        attend(LANES, c <= (r % n_t))
        o_ref[...] = (acc_ref[...] / l_ref[...]).astype(o_ref.dtype)


def _mla_decode(page_table, pool_c, pool_rt, q_lat, q_pe, c_new, rt_new, *, n_pg):
    n_seq, n_pages = page_table.shape
    _, page, dc = pool_c.shape
    dr = pool_rt.shape[1]
    rows = q_lat.shape[1]
    n_t = c_new.shape[1]
    n_chunk = n_pages // n_pg

    def cpage_spec(i):
        return pl.BlockSpec((None, page, dc), lambda s, j, pt: (pt[s, j * n_pg + i], 0, 0))

    def rpage_spec(i):
        return pl.BlockSpec((None, dr, page), lambda s, j, pt: (pt[s, j * n_pg + i], 0, 0))

    seq_spec = lambda shape: pl.BlockSpec((None,) + shape, lambda s, j, pt: (s,) + (0,) * len(shape))
    kern = functools.partial(_mla_decode_kernel, n_pg=n_pg, page=page, n_t=n_t)
    grid_spec = pltpu.PrefetchScalarGridSpec(
        num_scalar_prefetch=1,
        grid=(n_seq, n_chunk),
        in_specs=([cpage_spec(i) for i in range(n_pg)] + [rpage_spec(i) for i in range(n_pg)]
                  + [seq_spec((rows, dc)), seq_spec((rows, dr)), seq_spec((n_t, dc)), seq_spec((dr, n_t))]),
        out_specs=seq_spec((rows, dc)),
        scratch_shapes=[pltpu.VMEM((n_pg * page, dc), BF16), pltpu.VMEM((dr, n_pg * page), BF16),
                        pltpu.VMEM((rows, 1), F32), pltpu.VMEM((rows, 1), F32),
                        pltpu.VMEM((rows, dc), F32)],
    )
    return pl.pallas_call(
        kern,
        out_shape=jax.ShapeDtypeStruct((n_seq, rows, dc), BF16),
        grid_spec=grid_spec,
        compiler_params=_cparams(("arbitrary", "arbitrary")),
        name="mla_decode",
    )(page_table, *([pool_c] * n_pg), *([pool_rt] * n_pg), q_lat, q_pe, c_new, rt_new)


def _mla_proj_kernel(x_ref, cos_ref, sin_ref, cost_ref, sint_ref, g_ref, w1_ref, qg_ref, kg_ref,
                     wqn_ref, wqp_ref, wqr_ref, wuk_ref, wuv_ref, *out_refs, sample, n_heads, qscale, lora):
    cosp = cos_ref[...]
    sinp = sin_ref[...]
    hn = _rms(x_ref[...], g_ref[...]).astype(BF16)
    a = _dot(hn, w1_ref[...])
    cqn = _rms(a[:, :lora], qg_ref[...]).astype(BF16)
    ckvn = _rms(a[:, lora:2 * lora], kg_ref[...])
    kp = a[:, 2 * lora:2 * lora + LANES] * cosp + a[:, 2 * lora + LANES:] * sinp
    rope_w = kp.shape[1] // 2
    ckvb = ckvn.astype(BF16)
    if sample:
        ckv_ref, kpe_ref, ql_ref, qpe_ref = out_refs
        qn = _dot(cqn, wqn_ref[...])
        qp = _dot(cqn, wqp_ref[...])
        qr = _dot(cqn, wqr_ref[...])
    else:
        ckv_ref, kpe_ref, qt_ref, kc_ref, vt_ref = out_refs
        cost = cost_ref[...]
        sint = sint_ref[...]
        kn = _dot(ckvb, wuk_ref[...])
        vt = _dot_nt(wuv_ref[...], ckvb).astype(BF16)
        tile = vt_ref.shape[2]
        for t in range(vt_ref.shape[0]):
            vt_ref[t] = vt[:, t * tile:(t + 1) * tile]
        qnt = _dot_nt(wqn_ref[...], cqn)
        qpt = _dot_nt(wqp_ref[...], cqn)
        qrt = _dot_nt(wqr_ref[...], cqn)
    ckv_ref[...] = ckvn
    kpe_ref[...] = kp[:, :rope_w]
    for h in range(n_heads):
        hs = slice(h * LANES, (h + 1) * LANES)
        if sample:
            q_rope = (qp[:, hs] * cosp + qr[:, hs] * sinp) * qscale
            ql_ref[h] = (_dot_nt(qn[:, hs].astype(BF16), wuk_ref[:, hs]) * qscale).astype(BF16)
            qpe_ref[h] = q_rope[:, :rope_w].astype(BF16)
        else:
            qt_ref[h, 0:LANES, :] = (qnt[hs] * qscale).astype(BF16)
            qt_ref[h, LANES:, :] = ((qpt[hs] * cost + qrt[hs] * sint) * qscale).astype(BF16)
            kc_ref[h, :, 0:LANES] = kn[:, hs].astype(BF16)
            kc_ref[h, :, LANES:] = kp.astype(BF16)


def _mla_proj(x, rope, rope_t, g, w1, qg, kg, wqn, wqp, wqr, wuk, wuv, *, sample, qscale, tile):
    m, d = x.shape
    lora = qg.shape[1]
    n_heads = wuk.shape[1] // LANES
    rope_w = LANES // 2
    tm = _row_tile(m, 512)
    tile = min(tile, tm)
    cosp, sinp = rope
    cost, sint = rope_t
    assert cosp.shape[0] % tm == 0
    n_pos = cosp.shape[0] // tm
    row = lambda width: pl.BlockSpec((tm, width), lambda i: (i, 0))
    pos_spec = pl.BlockSpec((tm, LANES), lambda i: (i % n_pos, 0))
    post_spec = pl.BlockSpec((LANES, tm), lambda i: (0, i % n_pos))
    if sample:
        hrow = lambda width: pl.BlockSpec((n_heads, tm, width), lambda i: (0, i, 0))
        out_shape = (jax.ShapeDtypeStruct((m, lora), F32), jax.ShapeDtypeStruct((m, rope_w), F32),
                     jax.ShapeDtypeStruct((n_heads, m, lora), BF16),
                     jax.ShapeDtypeStruct((n_heads, m, rope_w), BF16))
        out_specs = (row(lora), row(rope_w), hrow(lora), hrow(rope_w))
    else:
        out_shape = (jax.ShapeDtypeStruct((m, lora), F32), jax.ShapeDtypeStruct((m, rope_w), F32),
                     jax.ShapeDtypeStruct((n_heads, 2 * LANES, m), BF16),
                     jax.ShapeDtypeStruct((n_heads, m, 2 * LANES), BF16),
                     jax.ShapeDtypeStruct((m // tile, n_heads * LANES, tile), BF16))
        out_specs = (row(lora), row(rope_w),
                     pl.BlockSpec((n_heads, 2 * LANES, tm), lambda i: (0, 0, i)),
                     pl.BlockSpec((n_heads, tm, 2 * LANES), lambda i: (0, i, 0)),
                     pl.BlockSpec((tm // tile, n_heads * LANES, tile), lambda i: (i, 0, 0)))
    kern = functools.partial(_mla_proj_kernel, sample=sample, n_heads=n_heads, qscale=qscale, lora=lora)
    return pl.pallas_call(
        kern,
        out_shape=out_shape,
        grid=(m // tm,),
        in_specs=[row(d), pos_spec, pos_spec, post_spec, post_spec, _const_spec(g.shape),
                  _const_spec(w1.shape), _const_spec(qg.shape), _const_spec(kg.shape),
                  _const_spec(wqn.shape), _const_spec(wqp.shape), _const_spec(wqr.shape),
                  _const_spec(wuk.shape), _const_spec(wuv.shape)],
        out_specs=out_specs,
        compiler_params=_cparams(("arbitrary",)),
        name="mla_proj_sample" if sample else "mla_proj_prompt",
    )(x, cosp, sinp, cost, sint, g, w1, qg, kg, wqn, wqp, wqr, wuk, wuv)


def _head_matmul_kernel(o_ref, w_ref, y_ref):
    y_ref[...] = _dot(o_ref[...], w_ref[...]).astype(y_ref.dtype)


def _head_matmul(o, w):
    n_heads, m, c = o.shape
    v = w.shape[2]
    return pl.pallas_call(
        _head_matmul_kernel,
        out_shape=jax.ShapeDtypeStruct((m, n_heads * v), BF16),
        grid=(n_heads,),
        in_specs=[pl.BlockSpec((None, m, c), lambda h: (h, 0, 0)),
                  pl.BlockSpec((None, c, v), lambda h: (h, 0, 0))],
        out_specs=pl.BlockSpec((m, v), lambda h: (0, h)),
        compiler_params=_cparams(("arbitrary",)),
        name="mla_value_up",
    )(o, w)


def _proj_res_kernel(o_ref, x_ref, w_ref, g_ref, y_ref):
    y = _dot(o_ref[...].astype(BF16), w_ref[...])
    y_ref[...] = x_ref[...] + _rms(y, g_ref[...])


def _proj_res(o, x, w, g):
    m, d = x.shape
    k = o.shape[1]
    tm = _row_tile(m, 512)
    return pl.pallas_call(
        _proj_res_kernel,
        out_shape=jax.ShapeDtypeStruct((m, d), F32),
        grid=(m // tm,),
        in_specs=[pl.BlockSpec((tm, k), lambda i: (i, 0)), pl.BlockSpec((tm, d), lambda i: (i, 0)),
                  _const_spec(w.shape), _const_spec(g.shape)],
        out_specs=pl.BlockSpec((tm, d), lambda i: (i, 0)),
        compiler_params=_cparams(("arbitrary",)),
        name="attn_out_proj",
    )(o, x, w, g)


def _ffn_kernel(x_ref, g2_ref, g3_ref, win_ref, cw_ref, cb_ref, wout_ref, st_ref, y_ref, cs_ref,
                ext_ref, h_ref, *, tm, ff, shift, pad, chunk):
    i = pl.program_id(1)
    lo = pad - 2 * shift

    @pl.when(i == 0)
    def _load_state():
        ext_ref[lo:pad, :] = st_ref[...]

    x = x_ref[...]
    xn = _rms(x, g2_ref[...]).astype(BF16)
    c0 = math.sqrt(2.0 / math.pi)
    for c in range(ff // chunk):
        cs = slice(c * chunk, (c + 1) * chunk)
        gate = _dot(xn, win_ref[:, c * chunk:(c + 1) * chunk])
        val = _dot(xn, win_ref[:, ff + c * chunk:ff + (c + 1) * chunk])
        ext_ref[pad:pad + tm, cs] = gate
        conv = (cb_ref[:, cs] + ext_ref[lo:lo + tm, cs] * cw_ref[0:1, cs]
                + ext_ref[lo + shift:lo + shift + tm, cs] * cw_ref[1:2, cs] + gate * cw_ref[2:3, cs])
        act = 0.5 * conv * (1.0 + jnp.tanh(c0 * (conv + 0.044715 * (conv * conv * conv))))
        h_ref[:, cs] = (act * val).astype(BF16)
        tail = ext_ref[tm + lo:tm + pad, cs]
        cs_ref[:, cs] = tail
        ext_ref[lo:pad, cs] = tail
    y = _dot(h_ref[...], wout_ref[...])
    y_ref[...] = x + _rms(y, g3_ref[...])


def _conv_ffn(x, g2, g3, w_in, conv_w, conv_b, w_out, state, *, n_seq, shift, tm_cap):
    m, d = x.shape
    ff = w_out.shape[0]
    seq_len = m // n_seq
    tm = _row_tile(seq_len, tm_cap)
    nb = seq_len // tm
    pad = max(8, 2 * shift)
    chunk = 256
    assert ff % chunk == 0 and tm >= 2 * shift
    kern = functools.partial(_ffn_kernel, tm=tm, ff=ff, shift=shift, pad=pad, chunk=chunk)
    const = lambda shape: pl.BlockSpec(shape, lambda s, i: (0,) * len(shape), pipeline_mode=pl.Buffered(1))
    return pl.pallas_call(
        kern,
        out_shape=(jax.ShapeDtypeStruct((m, d), F32), jax.ShapeDtypeStruct((n_seq, 2 * shift, ff), F32)),
        grid=(n_seq, nb),
        in_specs=[pl.BlockSpec((tm, d), lambda s, i: (s * nb + i, 0)), const(g2.shape), const(g3.shape),
                  const(w_in.shape), const(conv_w.shape), const(conv_b.shape), const(w_out.shape),
                  pl.BlockSpec((None, 2 * shift, ff), lambda s, i: (s, 0, 0))],
        out_specs=(pl.BlockSpec((tm, d), lambda s, i: (s * nb + i, 0)),
                   pl.BlockSpec((None, 2 * shift, ff), lambda s, i: (s, 0, 0))),
        scratch_shapes=[pltpu.VMEM((pad + tm, ff), F32), pltpu.VMEM((tm, ff), BF16)],
        compiler_params=_cparams(("arbitrary", "arbitrary")),
        name="conv_ffn",
    )(x, g2, g3, w_in, conv_w, conv_b, w_out, state)


def _rope_tables(pos):
    half = 32
    inv = jnp.power(ROPE_THETA, -jnp.arange(half, dtype=F32) / half)
    ang = pos.astype(F32)[:, None] * inv
    z = jnp.zeros((pos.shape[0], LANES - 2 * half), F32)
    cos, sin = jnp.cos(ang), jnp.sin(ang)
    return jnp.concatenate([cos, cos, z], axis=1), jnp.concatenate([sin, sin, z], axis=1)


def _rot_cols(w):
    half = w.shape[-1] // 2
    return jnp.concatenate([-w[..., half:], w[..., :half]], axis=-1)


def _pad_axis(a, size, axis):
    shape = list(a.shape)
    shape[axis] = size - a.shape[axis]
    return jnp.concatenate([a, jnp.zeros(shape, a.dtype)], axis=axis)


def kernel(x_prompt, x_sample, cache_k_a, cache_v_a, cache_ckv_b, cache_kpe_b, state_conv, page_table,
           meta_tokens, rel_bias, norm_g, w_qkv_a, lambda_q1, lambda_k1, lambda_q2, lambda_k2, subln_g, w_o_a,
           w_dqkv_b, q_norm_g, w_uq, kv_norm_g, w_uk, w_uv, w_o_b, w_ffn_in, conv_w, conv_b, w_ffn_out):
    n_b, seq, d = x_prompt.shape
    n_s, n_t, _ = x_sample.shape
    n_meta = meta_tokens.shape[0]
    depth = norm_g.shape[0]
    ff = w_ffn_out.shape[1]
    a_heads, a_dh = cache_k_a.shape[3], cache_k_a.shape[5]
    a_w = a_heads * 2 * a_dh
    n_pool, page = cache_k_a.shape[1], cache_k_a.shape[2]
    n_pages = page_table.shape[1]
    past = n_pages * page
    b_heads, b_nope = w_uk.shape[2], w_uk.shape[3]
    lora_q, lora_kv, b_rope = q_norm_g.shape[1], kv_norm_g.shape[1], cache_kpe_b.shape[3]
    mla_scale = (b_nope + b_rope) ** -0.5
    T = ATTN_TILE
    assert seq % (2 * T) == 0 and n_meta <= LANES and n_t <= 8 and b_nope == LANES and 2 * a_dh == LANES
    assert lora_q == lora_kv and b_rope == LANES // 2 and page == LANES

    xm = meta_tokens.astype(F32)
    xp = x_prompt.reshape(n_b * seq, d)
    xs = x_sample.transpose(1, 0, 2).reshape(n_t * n_s, d)

    pos_m = jnp.arange(n_meta, dtype=jnp.int32)
    pos_p = n_meta + jnp.arange(seq, dtype=jnp.int32)
    pos_s = jnp.repeat(past + jnp.arange(n_t, dtype=jnp.int32), n_s)
    rope_m, rope_p, rope_s = _rope_tables(pos_m), _rope_tables(pos_p), _rope_tables(pos_s)
    tabs_t = lambda tabs: tuple(t.T for t in tabs)

    def to_seq_major(a):
        return a.reshape(n_t, n_s, a.shape[-1]).transpose(1, 0, 2)

    def with_meta(meta_rows, main_rows):
        w = meta_rows.shape[-1]
        return jnp.concatenate([jnp.broadcast_to(meta_rows[None], (n_b, n_meta, w)),
                                main_rows.reshape(n_b, seq, w)], axis=1)

    outs = {k: [] for k in ("k_a_p", "v_a_p", "k_a_s", "v_a_s", "ckv_p", "kpe_p", "ckv_s", "kpe_s",
                            "conv_p", "conv_s")}
    for i in range(depth):
        g = norm_g[i].astype(F32)
        g0, g1, g2, g3 = (g[r:r + 1] for r in range(4))
        j = i // 2
        if i % 2 == 0:
            lam_init = 0.8 - 0.6 * math.exp(-0.3 * i)
            lam_vecs = jnp.stack([lambda_q1[j], lambda_k1[j], lambda_q2[j], lambda_k2[j]]).astype(F32)
            sg = subln_g[j].astype(F32)[None]
            w_qkv = w_qkv_a[j].astype(BF16)
            w_qv_t = jnp.concatenate([w_qkv[:, :a_w].T, w_qkv[:, 2 * a_w:].T], axis=0)
            qscale = a_dh ** -0.5 * LOG2E
            b31 = rel_bias[N_BUCKETS - 1].astype(F32) * LOG2E
            proj = functools.partial(_qkv_proj, g=g0, w=w_qkv, wt=w_qv_t, qscale=qscale, q_tile=2 * T,
                                     v_tile=T)
            km, vm, kmb, qmt, vmt = proj(xm, transposed=True)
            kp, vp, kpb, qpt, vpt = proj(xp, transposed=True)
            ks, vs, qs = proj(xs, transposed=False)
            kmb = _pad_axis(kmb, LANES, 0)
            qmt, vmt = _pad_axis(qmt, LANES, 2), _pad_axis(vmt[0], LANES, 1)
            bias_meta = _bias_tiles(rel_bias, LANES, ((0, LANES),), True)
            bias_main = _bias_tiles(rel_bias, T, ((-T, 2 * T), (0, 2 * T), (T, 2 * T), (-n_meta, 2 * T)), True)
            flash = functools.partial(_flash_diff, km=kmb, vmt=vmt, b31=b31, lam_vecs=lam_vecs,
                                      sg_col=sg.T, n_heads=a_heads, n_meta=n_meta, lam_init=lam_init)
            om = flash(qmt, kmb, vmt[None], bias=bias_meta, n_seq=1, has_main=False)[:n_meta]
            op = flash(qpt, kpb, vpt, bias=bias_main, n_seq=n_b, has_main=True)
            n_pg = min(16, n_pages)
            n_pair = a_heads // 2
            bias_dec = _bias_tiles(rel_bias, 8, ((-n_pg * page, n_pg * page), (0, LANES)), False)[:, :n_t]
            expand = lambda b: jnp.broadcast_to(b[:, None], (a_heads, 2, n_t, b.shape[-1])).reshape(
                a_heads * 2 * n_t, b.shape[-1])
            b_last = expand(bias_dec[:, :, :n_pg * page])
            col = jnp.arange(LANES)[None, None, :]
            b_new = expand(jnp.where((col < n_t) & (col <= jnp.arange(n_t)[None, :, None]),
                                     bias_dec[:, :, n_pg * page:], NEG_INF))
            b_far = jnp.repeat(b31, 2 * n_t)[:, None]
            q5 = to_seq_major(qs).reshape(n_s, n_t, n_pair, 4, a_dh)
            eye = jnp.eye(4, dtype=BF16)
            qbd = (q5.transpose(0, 2, 3, 1, 4)[:, :, :, :, None, :]
                   * eye[None, None, :, None, :, None]).reshape(n_s, n_pair, 4 * n_t, 4 * a_dh)
            pool_kt = cache_k_a[j].transpose(0, 2, 3, 4, 1).reshape(n_pool, a_w, page)
            pool_v = cache_v_a[j].reshape(n_pool, page * a_heads, 2 * a_dh)
            os_ = _diff_decode(page_table, pool_kt, pool_v, qbd, to_seq_major(ks).transpose(0, 2, 1),
                               to_seq_major(vs), b_far, b_last, b_new, lam_vecs, sg, n_pg=n_pg,
                               n_heads=a_heads, lam_init=lam_init)
            os_ = os_.transpose(1, 0, 2).reshape(n_t * n_s, a_w)
            w_o = w_o_a[j].astype(BF16)
            outs["k_a_p"].append(with_meta(km, kp).reshape(n_b, n_meta + seq, a_heads, 2, a_dh))
            outs["v_a_p"].append(with_meta(vm, vp).reshape(n_b, n_meta + seq, a_heads, 2 * a_dh))
            outs["k_a_s"].append(to_seq_major(ks).reshape(n_s, n_t, a_heads, 2, a_dh))
            outs["v_a_s"].append(to_seq_major(vs).reshape(n_s, n_t, a_heads, 2 * a_dh))
        else:
            wd = w_dqkv_b[j]
            w_kpe = wd[:, lora_q + lora_kv:]
            w1 = jnp.concatenate([wd[:, :lora_q + lora_kv], _pad_axis(w_kpe, LANES, 1),
                                  _pad_axis(_rot_cols(w_kpe), LANES, 1)], axis=1).astype(BF16)
            wq = w_uq[j].reshape(lora_q, b_heads, b_nope + b_rope)
            wqn = wq[:, :, :b_nope].reshape(lora_q, b_heads * b_nope).astype(BF16)
            wqp = _pad_axis(wq[:, :, b_nope:], LANES, 2).reshape(lora_q, b_heads * LANES).astype(BF16)
            wqr = _pad_axis(_rot_cols(wq[:, :, b_nope:]), LANES, 2).reshape(
                lora_q, b_heads * LANES).astype(BF16)
            wuk = w_uk[j].reshape(lora_kv, b_heads * b_nope).astype(BF16)
            wuv = w_uv[j].reshape(lora_kv, -1).astype(BF16)
            qg, kg = q_norm_g[j].astype(F32)[None], kv_norm_g[j].astype(F32)[None]
            qscale = mla_scale * LOG2E
            proj = functools.partial(_mla_proj, g=g0, w1=w1, qg=qg, kg=kg, wuk=wuk, qscale=qscale,
                                     tile=2 * T)
            proj_p = functools.partial(proj, wqn=wqn.T, wqp=wqp.T, wqr=wqr.T, wuv=wuv.T, sample=False)
            ckm, kpm, qmt, kcm, vmt = proj_p(xm, rope_m, tabs_t(rope_m))
            ckp, kpp, qpt, kcp, vpt = proj_p(xp, rope_p, tabs_t(rope_p))
            cks, kps, qls, qps = proj(xs, rope_s, tabs_t(rope_s), wqn=wqn, wqp=wqp, wqr=wqr, wuv=wuv,
                                      sample=True)
            kcm = _pad_axis(kcm, LANES, 1)
            qmt, vmt = _pad_axis(qmt, LANES, 2), _pad_axis(vmt[0], LANES, 1)
            om = _flash_mla(qmt, kcm, vmt[None], kcm, vmt, n_seq=1, has_main=False, n_meta=n_meta)[:n_meta]
            op = _flash_mla(qpt, kcp, vpt, kcm, vmt, n_seq=n_b, has_main=True, n_meta=n_meta)
            n_pg = n_pages
            hseq = lambda a: a.reshape(b_heads, n_t, n_s, a.shape[-1]).transpose(2, 0, 1, 3).reshape(
                n_s, b_heads * n_t, a.shape[-1])
            pool_rt = cache_kpe_b[j].transpose(0, 2, 1)
            ol = _mla_decode(page_table, cache_ckv_b[j], pool_rt, hseq(qls), hseq(qps),
                             to_seq_major(cks), to_seq_major(kps).transpose(0, 2, 1), n_pg=n_pg)
            ol = ol.reshape(n_s, b_heads, n_t, lora_kv).transpose(1, 2, 0, 3).reshape(
                b_heads, n_t * n_s, lora_kv)
            os_ = _head_matmul(ol, w_uv[j].transpose(1, 0, 2).astype(BF16))
            w_o = w_o_b[j].astype(BF16)
            outs["ckv_p"].append(with_meta(ckm, ckp))
            outs["kpe_p"].append(with_meta(kpm, kpp))
            outs["ckv_s"].append(to_seq_major(cks))
            outs["kpe_s"].append(to_seq_major(kps))
        xm = _proj_res(om, xm, w_o, g1)
        xp = _proj_res(op, xp, w_o, g1)
        xs = _proj_res(os_, xs, w_o, g1)

        w_in, w_out = w_ffn_in[i].astype(BF16), w_ffn_out[i].astype(BF16)
        cw, cb = conv_w[i].astype(F32), conv_b[i].astype(F32)[None]
        ffn = functools.partial(_conv_ffn, g2=g2, g3=g3, w_in=w_in, conv_w=cw, conv_b=cb, w_out=w_out)
        xm, cm = ffn(xm, state=jnp.zeros((1, 2, ff), F32), n_seq=1, shift=1, tm_cap=512)
        xp, cp = ffn(xp, state=jnp.broadcast_to(cm, (n_b, 2, ff)), n_seq=n_b, shift=1, tm_cap=512)
        st = state_conv[i].astype(F32).transpose(1, 0, 2).reshape(1, 2 * n_s, ff)
        xs, cs = ffn(xs, state=st, n_seq=1, shift=n_s, tm_cap=2 * n_s)
        outs["conv_p"].append(cp)
        outs["conv_s"].append(cs.reshape(2, n_s, ff).transpose(1, 0, 2))

    y_prompt = xp.reshape(n_b, seq, d)
    y_sample = to_seq_major(xs)
    return (y_prompt, y_sample,
            jnp.stack(outs["k_a_p"]), jnp.stack(outs["v_a_p"]), jnp.stack(outs["k_a_s"]),
            jnp.stack(outs["v_a_s"]), jnp.stack(outs["ckv_p"]), jnp.stack(outs["kpe_p"]),
            jnp.stack(outs["ckv_s"]), jnp.stack(outs["kpe_s"]), jnp.stack(outs["conv_p"]),
            jnp.stack(outs["conv_s"]))
```

```python
import functools
import math

import jax
import jax.numpy as jnp
from jax import lax
from jax.experimental import pallas as pl
from jax.experimental.pallas import tpu as pltpu

F32 = jnp.float32
BF16 = jnp.bfloat16

EPS = 1e-6
NEG_INF = -1e30
LOG2E = 1.4426950408889634
N_BUCKETS = 32
MAX_DISTANCE = 128
ROPE_THETA = 10000.0
LANES = 128
ATTN_TILE = 256
PAGE_SLOTS = 3
VMEM_LIMIT = 56 * 1024 * 1024


def _cparams(sem):
    return pltpu.CompilerParams(dimension_semantics=sem, vmem_limit_bytes=VMEM_LIMIT)


def _const_spec(shape):
    nd = len(shape)
    return pl.BlockSpec(shape, lambda *_: (0,) * nd, pipeline_mode=pl.Buffered(1))


def _rms(x, g):
    return x * lax.rsqrt(jnp.mean(x * x, axis=-1, keepdims=True) + EPS) * g


def _dot(a, b):
    return jnp.dot(a, b, preferred_element_type=F32)


def _dot_nt(a, b):
    return lax.dot_general(a, b, (((1,), (1,)), ((), ())), preferred_element_type=F32)


def _row_tile(m, cap):
    t = min(m, cap)
    while m % t:
        t //= 2
    return t


def _lambda(lam_ref, lam_init):
    lv = lam_ref[...]
    return (jnp.exp(jnp.sum(lv[0:1] * lv[1:2], axis=-1, keepdims=True))
            - jnp.exp(jnp.sum(lv[2:3] * lv[3:4], axis=-1, keepdims=True)) + lam_init)


def _bias_kernel(tbl_ref, o_ref, *, segs, keys_on_rows):
    h = pl.program_id(0)
    rows = o_ref.shape[0]
    col0 = 0
    max_exact = N_BUCKETS // 2
    for off, width in segs:
        r = lax.broadcasted_iota(jnp.int32, (rows, width), 0)
        c = lax.broadcasted_iota(jnp.int32, (rows, width), 1)
        rel = (r - c + off) if keys_on_rows else (c - r + off)
        n = jnp.maximum(-rel, 0)
        nf = jnp.maximum(n, 1).astype(F32)
        log_b = max_exact + (jnp.log(nf / max_exact) / math.log(MAX_DISTANCE / max_exact)
                             * (N_BUCKETS - max_exact)).astype(jnp.int32)
        bucket = jnp.where(n < max_exact, n, jnp.minimum(log_b, N_BUCKETS - 1))
        val = jnp.full((rows, width), tbl_ref[N_BUCKETS - 1, h], F32)
        for b in range(N_BUCKETS - 1):
            val = jnp.where(bucket == b, tbl_ref[b, h], val)
        o_ref[:, col0:col0 + width] = val * LOG2E
        col0 += width


def _bias_tiles(rel_bias, rows, segs, keys_on_rows):
    n_heads = rel_bias.shape[1]
    width = sum(w for _, w in segs)
    return pl.pallas_call(
        functools.partial(_bias_kernel, segs=segs, keys_on_rows=keys_on_rows),
        out_shape=jax.ShapeDtypeStruct((n_heads, rows, width), F32),
        grid=(n_heads,),
        in_specs=[pl.BlockSpec(memory_space=pltpu.SMEM)],
        out_specs=pl.BlockSpec((None, rows, width), lambda h: (h, 0, 0)),
        compiler_params=_cparams(("arbitrary",)),
        name="rel_bias_tiles",
    )(rel_bias)


def _qkv_kernel(x_ref, g_ref, w_ref, wt_ref, *out_refs, qscale, transposed):
    xn = _rms(x_ref[...], g_ref[...]).astype(BF16)
    if transposed:
        k_ref, v_ref, kb_ref, qt_ref, vt_ref = out_refs
    else:
        k_ref, v_ref, q_ref = out_refs
    n = k_ref.shape[1]
    k = _dot(xn, w_ref[:, n:2 * n])
    k_ref[...] = k
    v = _dot(xn, w_ref[:, 2 * n:])
    v_ref[...] = v
    if transposed:
        kb_ref[...] = k.astype(BF16)
        qt = (_dot_nt(wt_ref[0:n, :], xn) * qscale).astype(BF16)
        vt = _dot_nt(wt_ref[n:2 * n, :], xn).astype(BF16)
        for ref, val in ((qt_ref, qt), (vt_ref, vt)):
            tile = ref.shape[2]
            for t in range(ref.shape[0]):
                ref[t] = val[:, t * tile:(t + 1) * tile]
    else:
        q_ref[...] = (_dot(xn, w_ref[:, :n]) * qscale).astype(BF16)


def _qkv_proj(x, g, w, wt, qscale, *, transposed, q_tile, v_tile):
    m, d = x.shape
    n = w.shape[1] // 3
    tm = _row_tile(m, 512)
    row = lambda width: pl.BlockSpec((tm, width), lambda i: (i, 0))

    def t_out(tile):
        tile = min(tile, tm)
        return (jax.ShapeDtypeStruct((m // tile, n, tile), BF16),
                pl.BlockSpec((tm // tile, n, tile), lambda i: (i, 0, 0)))

    if transposed:
        (q_shape, q_spec), (v_shape, v_spec) = t_out(q_tile), t_out(v_tile)
        out_shape = (jax.ShapeDtypeStruct((m, n), F32), jax.ShapeDtypeStruct((m, n), F32),
                     jax.ShapeDtypeStruct((m, n), BF16), q_shape, v_shape)
        out_specs = (row(n), row(n), row(n), q_spec, v_spec)
    else:
        out_shape = (jax.ShapeDtypeStruct((m, n), F32), jax.ShapeDtypeStruct((m, n), F32),
                     jax.ShapeDtypeStruct((m, n), BF16))
        out_specs = (row(n), row(n), row(n))
    return pl.pallas_call(
        functools.partial(_qkv_kernel, qscale=qscale, transposed=transposed),
        out_shape=out_shape,
        grid=(m // tm,),
        in_specs=[row(d), _const_spec((1, d)), _const_spec(w.shape), _const_spec(wt.shape)],
        out_specs=out_specs,
        compiler_params=_cparams(("arbitrary",)),
        name="qkv_proj_t" if transposed else "qkv_proj",
    )(x, g, w, wt)


def _flash_kernel(*refs, diff, has_main, tq, n_meta, lam_init):
    if diff:
        (b31_ref, lam_ref, sg_ref, qt_ref, k_ref, vt_ref, km_ref, vmt_ref, bias_ref,
         o_ref, acc_ref, q2_ref) = refs
    else:
        qt_ref, k_ref, vt_ref, km_ref, vmt_ref, o_ref, acc_ref = refs
    h = pl.program_id(1)
    qi = pl.program_id(2)
    T = ATTN_TILE
    nq = acc_ref.shape[1]

    if diff:
        qt = qt_ref[...]
        row = lax.broadcasted_iota(jnp.int32, qt.shape, 0)
        half = qt.shape[0] // 2
        q2_ref[:, :tq] = jnp.where(row < half, qt, jnp.zeros_like(qt))
        q2_ref[:, tq:] = jnp.where(row >= half, qt, jnp.zeros_like(qt))
        qq = q2_ref[...]
        b31 = b31_ref[h]
    else:
        qq = qt_ref[...]
        b31 = None
    both = (lambda a: jnp.concatenate([a, a], axis=1)) if diff else (lambda a: a)
    acc_ref[...] = jnp.zeros(acc_ref.shape, F32)

    def scores(kt, bias, mask):
        s = _dot(kt, qq)
        if bias is not None:
            s = s + bias
        if mask is not None:
            s = jnp.where(mask, s, NEG_INF)
        return s

    def absorb(s, vtt, carry, shift=None):
        m_prev, l_prev = carry
        if shift is not None:
            m_prev = m_prev - shift
        m_new = jnp.maximum(m_prev, jnp.max(s, axis=0, keepdims=True))
        alpha = jnp.exp2(m_prev - m_new)
        p = jnp.exp2(s - m_new)
        l_new = alpha * l_prev + jnp.sum(p, axis=0, keepdims=True)
        acc_ref[...] = alpha * acc_ref[...] + _dot(vtt, p.astype(BF16))
        if shift is not None:
            m_new = m_new + shift
        return m_new, l_new

    carry = (jnp.full((1, nq), NEG_INF, F32), jnp.zeros((1, nq), F32))

    r = lax.broadcasted_iota(jnp.int32, (LANES, tq), 0)
    c = lax.broadcasted_iota(jnp.int32, (LANES, tq), 1)
    if has_main:
        meta_mask = r < n_meta
        meta_bias = both(jnp.where(qi == 0, bias_ref[0:LANES, 3 * tq:4 * tq], b31)) if diff else None
    else:
        meta_mask = (r < n_meta) & (r <= c)
        meta_bias = both(bias_ref[...]) if diff else None
    carry = absorb(scores(km_ref[...], meta_bias, both(meta_mask)), vmt_ref[...], carry)

    if has_main:
        assert tq == 2 * T
        fsub = vt_ref.shape[2] // T

        def pair(j, bias_a, mask_a, bias_b, mask_b, carry, shift_a=None, shift_b=None):
            off = pl.multiple_of(j * 2 * T, 2 * T)
            if fsub == 1:
                va, vb = vt_ref[2 * j], vt_ref[2 * j + 1]
            else:
                va, vb = vt_ref[j, :, 0:T], vt_ref[j, :, T:2 * T]
            sa = scores(k_ref[pl.ds(off, T), :], bias_a, mask_a)
            sb = scores(k_ref[pl.ds(pl.multiple_of(off + T, T), T), :], bias_b, mask_b)
            return absorb(sb, vb, absorb(sa, va, carry, shift_a), shift_b)

        n_far = qi - 1 if diff else qi
        carry = lax.fori_loop(0, jnp.maximum(n_far, 0),
                              lambda j, carry: pair(j, None, None, None, None, carry, b31, b31), carry)
        if diff:
            live = jnp.broadcast_to(qi >= 1, (T, nq))
            carry = pair(jnp.maximum(qi - 1, 0), None, live, both(bias_ref[:, 0:tq]), live, carry, b31)
        rr = lax.broadcasted_iota(jnp.int32, (T, tq), 0)
        cc = lax.broadcasted_iota(jnp.int32, (T, tq), 1)
        bias_a = both(bias_ref[:, tq:2 * tq]) if diff else None
        bias_b = both(bias_ref[:, 2 * tq:3 * tq]) if diff else None
        carry = pair(qi, bias_a, both(rr <= cc), bias_b, both(rr + T <= cc), carry)

    _, l_fin = carry
    a = acc_ref[...] / l_fin
    if diff:
        o = a[:, :tq] - _lambda(lam_ref, lam_init) * a[:, tq:]
        o = o * lax.rsqrt(jnp.mean(o * o, axis=0, keepdims=True) + EPS) * sg_ref[...] * (1.0 - lam_init)
    else:
        o = a
    o_ref[...] = o.T.astype(o_ref.dtype)


def _flash_diff(qt, kb, vt, km, vmt, bias, b31, lam_vecs, sg_col, *, n_seq, n_heads, has_main, n_meta,
                lam_init):
    hd = kb.shape[1]
    dv = hd // n_heads
    tq, tkv = qt.shape[2], vt.shape[2]
    seq_len = qt.shape[0] * tq // n_seq
    nq = seq_len // tq
    kern = functools.partial(_flash_kernel, diff=True, has_main=has_main, tq=tq, n_meta=n_meta,
                             lam_init=lam_init)
    if has_main:
        kspec = pl.BlockSpec((seq_len, dv), lambda b, h, i: (b, h))
        vspec = pl.BlockSpec((seq_len // tkv, dv, tkv), lambda b, h, i: (b, h, 0))
    else:
        kspec = pl.BlockSpec((LANES, dv), lambda b, h, i: (0, h))
        vspec = pl.BlockSpec((1, dv, LANES), lambda b, h, i: (0, h, 0))
    return pl.pallas_call(
        kern,
        out_shape=jax.ShapeDtypeStruct((n_seq * seq_len, hd), BF16),
        grid=(n_seq, n_heads, nq),
        in_specs=[pl.BlockSpec(memory_space=pltpu.SMEM), _const_spec(lam_vecs.shape),
                  _const_spec(sg_col.shape),
                  pl.BlockSpec((None, dv, tq), lambda b, h, i: (b * nq + i, h, 0)),
                  kspec, vspec,
                  pl.BlockSpec((LANES, dv), lambda b, h, i: (0, h)),
                  pl.BlockSpec((dv, LANES), lambda b, h, i: (h, 0)),
                  pl.BlockSpec((None,) + bias.shape[1:], lambda b, h, i: (h, 0, 0))],
        out_specs=pl.BlockSpec((tq, dv), lambda b, h, i: (b * nq + i, h)),
        scratch_shapes=[pltpu.VMEM((dv, 2 * tq), F32), pltpu.VMEM((dv, 2 * tq), BF16)],
        compiler_params=_cparams(("arbitrary", "arbitrary", "arbitrary")),
        name="diff_flash" if has_main else "diff_meta_attn",
    )(b31, lam_vecs, sg_col, qt, kb, vt, km, vmt, bias)


def _flash_mla(qt, kc, vt, km, vmt, *, n_seq, has_main, n_meta):
    T = ATTN_TILE
    n_heads, dq, m = qt.shape
    dv = vmt.shape[0] // n_heads
    seq_len = m // n_seq
    tkv = vt.shape[2]
    tq = 2 * T if has_main else seq_len
    nq = seq_len // tq
    kern = functools.partial(_flash_kernel, diff=False, has_main=has_main, tq=tq, n_meta=n_meta,
                             lam_init=0.0)
    if has_main:
        kspec = pl.BlockSpec((None, seq_len, dq), lambda b, h, i: (h, b, 0))
        vspec = pl.BlockSpec((seq_len // tkv, dv, tkv), lambda b, h, i: (b, h, 0))
    else:
        kspec = pl.BlockSpec((None, LANES, dq), lambda b, h, i: (h, 0, 0))
        vspec = pl.BlockSpec((1, dv, LANES), lambda b, h, i: (0, h, 0))
    return pl.pallas_call(
        kern,
        out_shape=jax.ShapeDtypeStruct((m, n_heads * dv), BF16),
        grid=(n_seq, n_heads, nq),
        in_specs=[pl.BlockSpec((None, dq, tq), lambda b, h, i: (h, 0, b * nq + i)),
                  kspec, vspec,
                  pl.BlockSpec((None, LANES, dq), lambda b, h, i: (h, 0, 0)),
                  pl.BlockSpec((dv, LANES), lambda b, h, i: (h, 0))],
        out_specs=pl.BlockSpec((tq, dv), lambda b, h, i: (b * nq + i, h)),
        scratch_shapes=[pltpu.VMEM((dv, tq), F32)],
        compiler_params=_cparams(("arbitrary", "arbitrary", "arbitrary")),
        name="mla_flash" if has_main else "mla_meta_attn",
    )(qt, kc, vt, km, vmt)


def _online_update(s, m_ref, l_ref):
    m_prev = m_ref[...]
    m_new = jnp.maximum(m_prev, jnp.max(s, axis=-1, keepdims=True))
    alpha = jnp.exp2(m_prev - m_new)
    p = jnp.exp2(s - m_new)
    l_ref[...] = alpha * l_ref[...] + jnp.sum(p, axis=-1, keepdims=True)
    m_ref[...] = m_new
    return alpha, p.astype(BF16)


def _diff_decode_kernel(pt_ref, *refs, n_pg, page, n_pair, n_heads, lam_init):
    (kt_hbm, v_hbm, q_ref, kn_ref, vn_ref, bfar_ref, blast_ref, bnew_ref, lam_ref, sg_ref,
     o_ref, kbuf, vbuf, sem, kc_ref, vc_ref, m_ref, l_ref, acc_ref) = refs
    j = pl.program_id(1)
    n_chunk = pl.num_programs(1)
    last = n_chunk - 1
    step = pl.program_id(0) * n_chunk + j
    n_steps = pl.num_programs(0) * n_chunk
    ahead = PAGE_SLOTS - 1

    def page_copies(t):
        s_t, j_t, slot = t // n_chunk, t % n_chunk, t % PAGE_SLOTS
        copies = []
        for i in range(n_pg):
            pg = pt_ref[s_t, j_t * n_pg + i]
            copies.append(pltpu.make_async_copy(kt_hbm.at[pg], kbuf.at[slot, i], sem.at[slot, i]))
            copies.append(pltpu.make_async_copy(v_hbm.at[pg], vbuf.at[slot, i], sem.at[slot, n_pg + i]))
        return copies

    @pl.when(step == 0)
    def _prefill():
        for t in range(ahead):
            @pl.when(t < n_steps)
            def _():
                for cp in page_copies(t):
                    cp.start()

    @pl.when(step + ahead < n_steps)
    def _request():
        for cp in page_copies(step + ahead):
            cp.start()

    for cp in page_copies(step):
        cp.wait()
    slot = step % PAGE_SLOTS
    k_pages = [kbuf.at[slot, i] for i in range(n_pg)]
    v_pages = [vbuf.at[slot, i] for i in range(n_pg)]
    rp = q_ref.shape[1]
    wp = q_ref.shape[2]
    n_t = kn_ref.shape[1]
    dv = wp // 2

    @pl.when(j == 0)
    def _init():
        m_ref[...] = jnp.full(m_ref.shape, NEG_INF, F32)
        l_ref[...] = jnp.zeros(l_ref.shape, F32)
        acc_ref[...] = jnp.zeros(acc_ref.shape, F32)

    def attend(n_keys, bias):
        s = jnp.concatenate(
            [_dot(q_ref[pr], kc_ref[pr * wp:(pr + 1) * wp, 0:n_keys]) for pr in range(n_pair)], axis=0)
        alpha, p = _online_update(s + bias, m_ref, l_ref)
        for pr in range(n_pair):
            rows = slice(pr * rp, (pr + 1) * rp)
            acc_ref[pr] = alpha[rows] * acc_ref[pr] + _dot(p[rows], vc_ref[0:n_keys, pr * wp:(pr + 1) * wp])

    for i in range(n_pg):
        kc_ref[:, i * page:(i + 1) * page] = k_pages[i][...].astype(BF16)
        for hh in range(n_heads):
            vc_ref[i * page:(i + 1) * page, hh * dv:(hh + 1) * dv] = (
                v_pages[i][pl.ds(hh, page, stride=n_heads), :].astype(BF16))
    attend(n_pg * page, jnp.where(j == last, blast_ref[...], bfar_ref[...]))

    @pl.when(j == last)
    def _finish():
        kc_ref[:, 0:LANES] = jnp.zeros((kc_ref.shape[0], LANES), BF16)
        vc_ref[0:LANES, :] = jnp.zeros((LANES, vc_ref.shape[1]), BF16)
        kc_ref[:, 0:n_t] = kn_ref[...].astype(BF16)
        vc_ref[0:n_t, :] = vn_ref[...].astype(BF16)
        attend(LANES, bnew_ref[...])
        lam = _lambda(lam_ref, lam_init)
        inv_l = 1.0 / l_ref[...]
        for pr in range(n_pair):
            a = acc_ref[pr] * inv_l[pr * rp:(pr + 1) * rp]
            for hl in range(2):
                r0 = hl * 2 * n_t
                o = (a[r0:r0 + n_t, hl * dv:(hl + 1) * dv]
                     - lam * a[r0 + n_t:r0 + 2 * n_t, hl * dv:(hl + 1) * dv])
                o = _rms(o, sg_ref[...]) * (1.0 - lam_init)
                hcol = (2 * pr + hl) * dv
                o_ref[:, hcol:hcol + dv] = o


def _diff_decode(page_table, pool_kt, pool_v, qbd, kt_new, v_new, b_far, b_last, b_new, lam_vecs, subln_g,
                 *, n_pg, n_heads, lam_init):
    n_seq, n_pages = page_table.shape
    _, width, page = pool_kt.shape
    dv = pool_v.shape[2]
    n_pair, rp, wp = qbd.shape[1:]
    n_t = v_new.shape[1]
    n_chunk = n_pages // n_pg
    n_rows = n_pair * rp

    seq_spec = lambda shape: pl.BlockSpec((None,) + shape, lambda s, j, pt: (s,) + (0,) * len(shape))
    const = lambda shape: pl.BlockSpec(shape, lambda s, j, pt: (0,) * len(shape))
    kern = functools.partial(_diff_decode_kernel, n_pg=n_pg, page=page, n_pair=n_pair, n_heads=n_heads,
                             lam_init=lam_init)
    grid_spec = pltpu.PrefetchScalarGridSpec(
        num_scalar_prefetch=1,
        grid=(n_seq, n_chunk),
        in_specs=([pl.BlockSpec(memory_space=pl.ANY), pl.BlockSpec(memory_space=pl.ANY)]
                  + [seq_spec((n_pair, rp, wp)), seq_spec((width, n_t)), seq_spec((n_t, width)),
                     const(b_far.shape), const(b_last.shape), const(b_new.shape),
                     const(lam_vecs.shape), const(subln_g.shape)]),
        out_specs=seq_spec((n_t, width)),
        scratch_shapes=[pltpu.VMEM((PAGE_SLOTS, n_pg, width, page), F32),
                        pltpu.VMEM((PAGE_SLOTS, n_pg, page * n_heads, dv), F32),
                        pltpu.SemaphoreType.DMA((PAGE_SLOTS, 2 * n_pg)),
                        pltpu.VMEM((width, n_pg * page), BF16), pltpu.VMEM((n_pg * page, width), BF16),
                        pltpu.VMEM((n_rows, 1), F32), pltpu.VMEM((n_rows, 1), F32),
                        pltpu.VMEM((n_pair, rp, wp), F32)],
    )
    return pl.pallas_call(
        kern,
        out_shape=jax.ShapeDtypeStruct((n_seq, n_t, width), F32),
        grid_spec=grid_spec,
        compiler_params=_cparams(("arbitrary", "arbitrary")),
        name="diff_decode",
    )(page_table, pool_kt, pool_v, qbd, kt_new, v_new, b_far, b_last, b_new, lam_vecs, subln_g)


def _mla_decode_kernel(pt_ref, *refs, n_pg, page, n_t):
    del pt_ref
    c_pages = refs[:n_pg]
    r_pages = refs[n_pg:2 * n_pg]
    (ql_ref, qp_ref, cn_ref, rn_ref, o_ref, cc_ref, rc_ref, m_ref, l_ref, acc_ref) = refs[2 * n_pg:]
    j = pl.program_id(1)
    last = pl.num_programs(1) - 1

    @pl.when(j == 0)
    def _init():
        m_ref[...] = jnp.full(m_ref.shape, NEG_INF, F32)
        l_ref[...] = jnp.zeros(l_ref.shape, F32)
        acc_ref[...] = jnp.zeros(acc_ref.shape, F32)

    def attend(n_keys, mask):
        ckv = cc_ref[0:n_keys, :]
        s = _dot_nt(ql_ref[...], ckv) + _dot(qp_ref[...], rc_ref[:, 0:n_keys])
        if mask is not None:
            s = jnp.where(mask, s, NEG_INF)
        alpha, p = _online_update(s, m_ref, l_ref)
        acc_ref[...] = alpha * acc_ref[...] + _dot(p, ckv)

    for i in range(n_pg):
        cc_ref[i * page:(i + 1) * page, :] = c_pages[i][...].astype(BF16)
        rc_ref[:, i * page:(i + 1) * page] = r_pages[i][...].astype(BF16)
    attend(n_pg * page, None)

    @pl.when(j == last)
    def _finish():
        cc_ref[0:LANES, :] = jnp.zeros((LANES, cc_ref.shape[1]), BF16)
        rc_ref[:, 0:LANES] = jnp.zeros((rc_ref.shape[0], LANES), BF16)
        cc_ref[0:n_t, :] = cn_ref[...].astype(BF16)
        rc_ref[:, 0:n_t] = rn_ref[...].astype(BF16)
        rows = ql_ref.shape[0]
        r = lax.broadcasted_iota(jnp.int32, (rows, LANES), 0)
        c = lax.broadcasted_iota(jnp.int32, (rows, LANES), 1)
        attend(LANES, c <= (r % n_t))
        o_ref[...] = (acc_ref[...] / l_ref[...]).astype(o_ref.dtype)


def _mla_decode(page_table, pool_c, pool_rt, q_lat, q_pe, c_new, rt_new, *, n_pg):
    n_seq, n_pages = page_table.shape
    _, page, dc = pool_c.shape
    dr = pool_rt.shape[1]
    rows = q_lat.shape[1]
    n_t = c_new.shape[1]
    n_chunk = n_pages // n_pg

    def cpage_spec(i):
        return pl.BlockSpec((None, page, dc), lambda s, j, pt: (pt[s, j * n_pg + i], 0, 0))

    def rpage_spec(i):
        return pl.BlockSpec((None, dr, page), lambda s, j, pt: (pt[s, j * n_pg + i], 0, 0))

    seq_spec = lambda shape: pl.BlockSpec((None,) + shape, lambda s, j, pt: (s,) + (0,) * len(shape))
    kern = functools.partial(_mla_decode_kernel, n_pg=n_pg, page=page, n_t=n_t)
    grid_spec = pltpu.PrefetchScalarGridSpec(
        num_scalar_prefetch=1,
        grid=(n_seq, n_chunk),
        in_specs=([cpage_spec(i) for i in range(n_pg)] + [rpage_spec(i) for i in range(n_pg)]
                  + [seq_spec((rows, dc)), seq_spec((rows, dr)), seq_spec((n_t, dc)), seq_spec((dr, n_t))]),
        out_specs=seq_spec((rows, dc)),
        scratch_shapes=[pltpu.VMEM((n_pg * page, dc), BF16), pltpu.VMEM((dr, n_pg * page), BF16),
                        pltpu.VMEM((rows, 1), F32), pltpu.VMEM((rows, 1), F32),
                        pltpu.VMEM((rows, dc), F32)],
    )
    return pl.pallas_call(
        kern,
        out_shape=jax.ShapeDtypeStruct((n_seq, rows, dc), BF16),
        grid_spec=grid_spec,
        compiler_params=_cparams(("arbitrary", "arbitrary")),
        name="mla_decode",
    )(page_table, *([pool_c] * n_pg), *([pool_rt] * n_pg), q_lat, q_pe, c_new, rt_new)


def _mla_proj_kernel(x_ref, cos_ref, sin_ref, cost_ref, sint_ref, g_ref, w1_ref, qg_ref, kg_ref,
                     wqn_ref, wqp_ref, wqr_ref, wuk_ref, wuv_ref, *out_refs, sample, n_heads, qscale, lora):
    cosp = cos_ref[...]
    sinp = sin_ref[...]
    hn = _rms(x_ref[...], g_ref[...]).astype(BF16)
    a = _dot(hn, w1_ref[...])
    cqn = _rms(a[:, :lora], qg_ref[...]).astype(BF16)
    ckvn = _rms(a[:, lora:2 * lora], kg_ref[...])
    kp = a[:, 2 * lora:2 * lora + LANES] * cosp + a[:, 2 * lora + LANES:] * sinp
    rope_w = kp.shape[1] // 2
    ckvb = ckvn.astype(BF16)
    if sample:
        ckv_ref, kpe_ref, ql_ref, qpe_ref = out_refs
        qn = _dot(cqn, wqn_ref[...])
        qp = _dot(cqn, wqp_ref[...])
        qr = _dot(cqn, wqr_ref[...])
    else:
        ckv_ref, kpe_ref, qt_ref, kc_ref, vt_ref = out_refs
        cost = cost_ref[...]
        sint = sint_ref[...]
        kn = _dot(ckvb, wuk_ref[...])
        vt = _dot_nt(wuv_ref[...], ckvb).astype(BF16)
        tile = vt_ref.shape[2]
        for t in range(vt_ref.shape[0]):
            vt_ref[t] = vt[:, t * tile:(t + 1) * tile]
        qnt = _dot_nt(wqn_ref[...], cqn)
        qpt = _dot_nt(wqp_ref[...], cqn)
        qrt = _dot_nt(wqr_ref[...], cqn)
    ckv_ref[...] = ckvn
    kpe_ref[...] = kp[:, :rope_w]
    for h in range(n_heads):
        hs = slice(h * LANES, (h + 1) * LANES)
        if sample:
            q_rope = (qp[:, hs] * cosp + qr[:, hs] * sinp) * qscale
            ql_ref[h] = (_dot_nt(qn[:, hs].astype(BF16), wuk_ref[:, hs]) * qscale).astype(BF16)
            qpe_ref[h] = q_rope[:, :rope_w].astype(BF16)
        else:
            qt_ref[h, 0:LANES, :] = (qnt[hs] * qscale).astype(BF16)
            qt_ref[h, LANES:, :] = ((qpt[hs] * cost + qrt[hs] * sint) * qscale).astype(BF16)
            kc_ref[h, :, 0:LANES] = kn[:, hs].astype(BF16)
            kc_ref[h, :, LANES:] = kp.astype(BF16)


def _mla_proj(x, rope, rope_t, g, w1, qg, kg, wqn, wqp, wqr, wuk, wuv, *, sample, qscale, tile):
    m, d = x.shape
    lora = qg.shape[1]
    n_heads = wuk.shape[1] // LANES
    rope_w = LANES // 2
    tm = _row_tile(m, 512)
    tile = min(tile, tm)
    cosp, sinp = rope
    cost, sint = rope_t
    assert cosp.shape[0] % tm == 0
    n_pos = cosp.shape[0] // tm
    row = lambda width: pl.BlockSpec((tm, width), lambda i: (i, 0))
    pos_spec = pl.BlockSpec((tm, LANES), lambda i: (i % n_pos, 0))
    post_spec = pl.BlockSpec((LANES, tm), lambda i: (0, i % n_pos))
    if sample:
        hrow = lambda width: pl.BlockSpec((n_heads, tm, width), lambda i: (0, i, 0))
        out_shape = (jax.ShapeDtypeStruct((m, lora), F32), jax.ShapeDtypeStruct((m, rope_w), F32),
                     jax.ShapeDtypeStruct((n_heads, m, lora), BF16),
                     jax.ShapeDtypeStruct((n_heads, m, rope_w), BF16))
        out_specs = (row(lora), row(rope_w), hrow(lora), hrow(rope_w))
    else:
        out_shape = (jax.ShapeDtypeStruct((m, lora), F32), jax.ShapeDtypeStruct((m, rope_w), F32),
                     jax.ShapeDtypeStruct((n_heads, 2 * LANES, m), BF16),
                     jax.ShapeDtypeStruct((n_heads, m, 2 * LANES), BF16),
                     jax.ShapeDtypeStruct((m // tile, n_heads * LANES, tile), BF16))
        out_specs = (row(lora), row(rope_w),
                     pl.BlockSpec((n_heads, 2 * LANES, tm), lambda i: (0, 0, i)),
                     pl.BlockSpec((n_heads, tm, 2 * LANES), lambda i: (0, i, 0)),
                     pl.BlockSpec((tm // tile, n_heads * LANES, tile), lambda i: (i, 0, 0)))
    kern = functools.partial(_mla_proj_kernel, sample=sample, n_heads=n_heads, qscale=qscale, lora=lora)
    return pl.pallas_call(
        kern,
        out_shape=out_shape,
        grid=(m // tm,),
        in_specs=[row(d), pos_spec, pos_spec, post_spec, post_spec, _const_spec(g.shape),
                  _const_spec(w1.shape), _const_spec(qg.shape), _const_spec(kg.shape),
                  _const_spec(wqn.shape), _const_spec(wqp.shape), _const_spec(wqr.shape),
                  _const_spec(wuk.shape), _const_spec(wuv.shape)],
        out_specs=out_specs,
        compiler_params=_cparams(("arbitrary",)),
        name="mla_proj_sample" if sample else "mla_proj_prompt",
    )(x, cosp, sinp, cost, sint, g, w1, qg, kg, wqn, wqp, wqr, wuk, wuv)


def _head_matmul_kernel(o_ref, w_ref, y_ref):
    y_ref[...] = _dot(o_ref[...], w_ref[...]).astype(y_ref.dtype)


def _head_matmul(o, w):
    n_heads, m, c = o.shape
    v = w.shape[2]
    return pl.pallas_call(
        _head_matmul_kernel,
        out_shape=jax.ShapeDtypeStruct((m, n_heads * v), BF16),
        grid=(n_heads,),
        in_specs=[pl.BlockSpec((None, m, c), lambda h: (h, 0, 0)),
                  pl.BlockSpec((None, c, v), lambda h: (h, 0, 0))],
        out_specs=pl.BlockSpec((m, v), lambda h: (0, h)),
        compiler_params=_cparams(("arbitrary",)),
        name="mla_value_up",
    )(o, w)


def _proj_res_kernel(o_ref, x_ref, w_ref, g_ref, y_ref):
    y = _dot(o_ref[...].astype(BF16), w_ref[...])
    y_ref[...] = x_ref[...] + _rms(y, g_ref[...])


def _proj_res(o, x, w, g):
    m, d = x.shape
    k = o.shape[1]
    tm = _row_tile(m, 512)
    return pl.pallas_call(
        _proj_res_kernel,
        out_shape=jax.ShapeDtypeStruct((m, d), F32),
        grid=(m // tm,),
        in_specs=[pl.BlockSpec((tm, k), lambda i: (i, 0)), pl.BlockSpec((tm, d), lambda i: (i, 0)),
                  _const_spec(w.shape), _const_spec(g.shape)],
        out_specs=pl.BlockSpec((tm, d), lambda i: (i, 0)),
        compiler_params=_cparams(("arbitrary",)),
        name="attn_out_proj",
    )(o, x, w, g)


def _ffn_kernel(x_ref, g2_ref, g3_ref, win_ref, cw_ref, cb_ref, wout_ref, st_ref, y_ref, cs_ref,
                ext_ref, h_ref, *, tm, ff, shift, pad, chunk):
    i = pl.program_id(1)
    lo = pad - 2 * shift

    @pl.when(i == 0)
    def _load_state():
        ext_ref[lo:pad, :] = st_ref[...]

    x = x_ref[...]
    xn = _rms(x, g2_ref[...]).astype(BF16)
    c0 = math.sqrt(2.0 / math.pi)
    for c in range(ff // chunk):
        cs = slice(c * chunk, (c + 1) * chunk)
        gate = _dot(xn, win_ref[:, c * chunk:(c + 1) * chunk])
        val = _dot(xn, win_ref[:, ff + c * chunk:ff + (c + 1) * chunk])
        ext_ref[pad:pad + tm, cs] = gate
        conv = (cb_ref[:, cs] + ext_ref[lo:lo + tm, cs] * cw_ref[0:1, cs]
                + ext_ref[lo + shift:lo + shift + tm, cs] * cw_ref[1:2, cs] + gate * cw_ref[2:3, cs])
        act = 0.5 * conv * (1.0 + jnp.tanh(c0 * (conv + 0.044715 * (conv * conv * conv))))
        h_ref[:, cs] = (act * val).astype(BF16)
        tail = ext_ref[tm + lo:tm + pad, cs]
        cs_ref[:, cs] = tail
        ext_ref[lo:pad, cs] = tail
    y = _dot(h_ref[...], wout_ref[...])
    y_ref[...] = x + _rms(y, g3_ref[...])


def _conv_ffn(x, g2, g3, w_in, conv_w, conv_b, w_out, state, *, n_seq, shift, tm_cap):
    m, d = x.shape
    ff = w_out.shape[0]
    seq_len = m // n_seq
    tm = _row_tile(seq_len, tm_cap)
    nb = seq_len // tm
    pad = max(8, 2 * shift)
    chunk = 256
    assert ff % chunk == 0 and tm >= 2 * shift
    kern = functools.partial(_ffn_kernel, tm=tm, ff=ff, shift=shift, pad=pad, chunk=chunk)
    const = lambda shape: pl.BlockSpec(shape, lambda s, i: (0,) * len(shape), pipeline_mode=pl.Buffered(1))
    return pl.pallas_call(
        kern,
        out_shape=(jax.ShapeDtypeStruct((m, d), F32), jax.ShapeDtypeStruct((n_seq, 2 * shift, ff), F32)),
        grid=(n_seq, nb),
        in_specs=[pl.BlockSpec((tm, d), lambda s, i: (s * nb + i, 0)), const(g2.shape), const(g3.shape),
                  const(w_in.shape), const(conv_w.shape), const(conv_b.shape), const(w_out.shape),
                  pl.BlockSpec((None, 2 * shift, ff), lambda s, i: (s, 0, 0))],
        out_specs=(pl.BlockSpec((tm, d), lambda s, i: (s * nb + i, 0)),
                   pl.BlockSpec((None, 2 * shift, ff), lambda s, i: (s, 0, 0))),
        scratch_shapes=[pltpu.VMEM((pad + tm, ff), F32), pltpu.VMEM((tm, ff), BF16)],
        compiler_params=_cparams(("arbitrary", "arbitrary")),
        name="conv_ffn",
    )(x, g2, g3, w_in, conv_w, conv_b, w_out, state)


def _rope_tables(pos):
    half = 32
    inv = jnp.power(ROPE_THETA, -jnp.arange(half, dtype=F32) / half)
    ang = pos.astype(F32)[:, None] * inv
    z = jnp.zeros((pos.shape[0], LANES - 2 * half), F32)
    cos, sin = jnp.cos(ang), jnp.sin(ang)
    return jnp.concatenate([cos, cos, z], axis=1), jnp.concatenate([sin, sin, z], axis=1)


def _rot_cols(w):
    half = w.shape[-1] // 2
    return jnp.concatenate([-w[..., half:], w[..., :half]], axis=-1)


def _pad_axis(a, size, axis):
    shape = list(a.shape)
    shape[axis] = size - a.shape[axis]
    return jnp.concatenate([a, jnp.zeros(shape, a.dtype)], axis=axis)


def kernel(x_prompt, x_sample, cache_k_a, cache_v_a, cache_ckv_b, cache_kpe_b, state_conv, page_table,
           meta_tokens, rel_bias, norm_g, w_qkv_a, lambda_q1, lambda_k1, lambda_q2, lambda_k2, subln_g, w_o_a,
           w_dqkv_b, q_norm_g, w_uq, kv_norm_g, w_uk, w_uv, w_o_b, w_ffn_in, conv_w, conv_b, w_ffn_out):
    n_b, seq, d = x_prompt.shape
    n_s, n_t, _ = x_sample.shape
    n_meta = meta_tokens.shape[0]
    depth = norm_g.shape[0]
    ff = w_ffn_out.shape[1]
    a_heads, a_dh = cache_k_a.shape[3], cache_k_a.shape[5]
    a_w = a_heads * 2 * a_dh
    n_pool, page = cache_k_a.shape[1], cache_k_a.shape[2]
    n_pages = page_table.shape[1]
    past = n_pages * page
    b_heads, b_nope = w_uk.shape[2], w_uk.shape[3]
    lora_q, lora_kv, b_rope = q_norm_g.shape[1], kv_norm_g.shape[1], cache_kpe_b.shape[3]
    mla_scale = (b_nope + b_rope) ** -0.5
    T = ATTN_TILE
    assert seq % (2 * T) == 0 and n_meta <= LANES and n_t <= 8 and b_nope == LANES and 2 * a_dh == LANES
    assert lora_q == lora_kv and b_rope == LANES // 2 and page == LANES

    xm = meta_tokens.astype(F32)
    xp = x_prompt.reshape(n_b * seq, d)
    xs = x_sample.transpose(1, 0, 2).reshape(n_t * n_s, d)

    pos_m = jnp.arange(n_meta, dtype=jnp.int32)
    pos_p = n_meta + jnp.arange(seq, dtype=jnp.int32)
    pos_s = jnp.repeat(past + jnp.arange(n_t, dtype=jnp.int32), n_s)
    rope_m, rope_p, rope_s = _rope_tables(pos_m), _rope_tables(pos_p), _rope_tables(pos_s)
    tabs_t = lambda tabs: tuple(t.T for t in tabs)

    def to_seq_major(a):
        return a.reshape(n_t, n_s, a.shape[-1]).transpose(1, 0, 2)

    def with_meta(meta_rows, main_rows):
        w = meta_rows.shape[-1]
        return jnp.concatenate([jnp.broadcast_to(meta_rows[None], (n_b, n_meta, w)),
                                main_rows.reshape(n_b, seq, w)], axis=1)

    outs = {k: [] for k in ("k_a_p", "v_a_p", "k_a_s", "v_a_s", "ckv_p", "kpe_p", "ckv_s", "kpe_s",
                            "conv_p", "conv_s")}
    for i in range(depth):
        g = norm_g[i].astype(F32)
        g0, g1, g2, g3 = (g[r:r + 1] for r in range(4))
        j = i // 2
        if i % 2 == 0:
            lam_init = 0.8 - 0.6 * math.exp(-0.3 * i)
            lam_vecs = jnp.stack([lambda_q1[j], lambda_k1[j], lambda_q2[j], lambda_k2[j]]).astype(F32)
            sg = subln_g[j].astype(F32)[None]
            w_qkv = w_qkv_a[j].astype(BF16)
            w_qv_t = jnp.concatenate([w_qkv[:, :a_w].T, w_qkv[:, 2 * a_w:].T], axis=0)
            qscale = a_dh ** -0.5 * LOG2E
            b31 = rel_bias[N_BUCKETS - 1].astype(F32) * LOG2E
            proj = functools.partial(_qkv_proj, g=g0, w=w_qkv, wt=w_qv_t, qscale=qscale, q_tile=2 * T,
                                     v_tile=T)
            km, vm, kmb, qmt, vmt = proj(xm, transposed=True)
            kp, vp, kpb, qpt, vpt = proj(xp, transposed=True)
            ks, vs, qs = proj(xs, transposed=False)
            kmb = _pad_axis(kmb, LANES, 0)
            qmt, vmt = _pad_axis(qmt, LANES, 2), _pad_axis(vmt[0], LANES, 1)
            bias_meta = _bias_tiles(rel_bias, LANES, ((0, LANES),), True)
            bias_main = _bias_tiles(rel_bias, T, ((-T, 2 * T), (0, 2 * T), (T, 2 * T), (-n_meta, 2 * T)), True)
            flash = functools.partial(_flash_diff, km=kmb, vmt=vmt, b31=b31, lam_vecs=lam_vecs,
                                      sg_col=sg.T, n_heads=a_heads, n_meta=n_meta, lam_init=lam_init)
            om = flash(qmt, kmb, vmt[None], bias=bias_meta, n_seq=1, has_main=False)[:n_meta]
            op = flash(qpt, kpb, vpt, bias=bias_main, n_seq=n_b, has_main=True)
            n_pg = min(8, n_pages)
            n_pair = a_heads // 2
            bias_dec = _bias_tiles(rel_bias, 8, ((-n_pg * page, n_pg * page), (0, LANES)), False)[:, :n_t]
            expand = lambda b: jnp.broadcast_to(b[:, None], (a_heads, 2, n_t, b.shape[-1])).reshape(
                a_heads * 2 * n_t, b.shape[-1])
            b_last = expand(bias_dec[:, :, :n_pg * page])
            col = jnp.arange(LANES)[None, None, :]
            b_new = expand(jnp.where((col < n_t) & (col <= jnp.arange(n_t)[None, :, None]),
                                     bias_dec[:, :, n_pg * page:], NEG_INF))
            b_far = jnp.repeat(b31, 2 * n_t)[:, None]
            q5 = to_seq_major(qs).reshape(n_s, n_t, n_pair, 4, a_dh)
            eye = jnp.eye(4, dtype=BF16)
            qbd = (q5.transpose(0, 2, 3, 1, 4)[:, :, :, :, None, :]
                   * eye[None, None, :, None, :, None]).reshape(n_s, n_pair, 4 * n_t, 4 * a_dh)
            pool_kt = cache_k_a[j].transpose(0, 2, 3, 4, 1).reshape(n_pool, a_w, page)
            pool_v = cache_v_a[j].reshape(n_pool, page * a_heads, 2 * a_dh)
            os_ = _diff_decode(page_table, pool_kt, pool_v, qbd, to_seq_major(ks).transpose(0, 2, 1),
                               to_seq_major(vs), b_far, b_last, b_new, lam_vecs, sg, n_pg=n_pg,
                               n_heads=a_heads, lam_init=lam_init)
            os_ = os_.transpose(1, 0, 2).reshape(n_t * n_s, a_w)
            w_o = w_o_a[j].astype(BF16)
            outs["k_a_p"].append(with_meta(km, kp).reshape(n_b, n_meta + seq, a_heads, 2, a_dh))
            outs["v_a_p"].append(with_meta(vm, vp).reshape(n_b, n_meta + seq, a_heads, 2 * a_dh))
            outs["k_a_s"].append(to_seq_major(ks).reshape(n_s, n_t, a_heads, 2, a_dh))
            outs["v_a_s"].append(to_seq_major(vs).reshape(n_s, n_t, a_heads, 2 * a_dh))
        else:
            wd = w_dqkv_b[j]
            w_kpe = wd[:, lora_q + lora_kv:]
            w1 = jnp.concatenate([wd[:, :lora_q + lora_kv], _pad_axis(w_kpe, LANES, 1),
                                  _pad_axis(_rot_cols(w_kpe), LANES, 1)], axis=1).astype(BF16)
            wq = w_uq[j].reshape(lora_q, b_heads, b_nope + b_rope)
            wqn = wq[:, :, :b_nope].reshape(lora_q, b_heads * b_nope).astype(BF16)
            wqp = _pad_axis(wq[:, :, b_nope:], LANES, 2).reshape(lora_q, b_heads * LANES).astype(BF16)
            wqr = _pad_axis(_rot_cols(wq[:, :, b_nope:]), LANES, 2).reshape(
                lora_q, b_heads * LANES).astype(BF16)
            wuk = w_uk[j].reshape(lora_kv, b_heads * b_nope).astype(BF16)
            wuv = w_uv[j].reshape(lora_kv, -1).astype(BF16)
            qg, kg = q_norm_g[j].astype(F32)[None], kv_norm_g[j].astype(F32)[None]
            qscale = mla_scale * LOG2E
            proj = functools.partial(_mla_proj, g=g0, w1=w1, qg=qg, kg=kg, wuk=wuk, qscale=qscale,
                                     tile=2 * T)
            proj_p = functools.partial(proj, wqn=wqn.T, wqp=wqp.T, wqr=wqr.T, wuv=wuv.T, sample=False)
            ckm, kpm, qmt, kcm, vmt = proj_p(xm, rope_m, tabs_t(rope_m))
            ckp, kpp, qpt, kcp, vpt = proj_p(xp, rope_p, tabs_t(rope_p))
            cks, kps, qls, qps = proj(xs, rope_s, tabs_t(rope_s), wqn=wqn, wqp=wqp, wqr=wqr, wuv=wuv,
                                      sample=True)
            kcm = _pad_axis(kcm, LANES, 1)
            qmt, vmt = _pad_axis(qmt, LANES, 2), _pad_axis(vmt[0], LANES, 1)
            om = _flash_mla(qmt, kcm, vmt[None], kcm, vmt, n_seq=1, has_main=False, n_meta=n_meta)[:n_meta]
            op = _flash_mla(qpt, kcp, vpt, kcm, vmt, n_seq=n_b, has_main=True, n_meta=n_meta)
            n_pg = n_pages
            hseq = lambda a: a.reshape(b_heads, n_t, n_s, a.shape[-1]).transpose(2, 0, 1, 3).reshape(
                n_s, b_heads * n_t, a.shape[-1])
            pool_rt = cache_kpe_b[j].transpose(0, 2, 1)
            ol = _mla_decode(page_table, cache_ckv_b[j], pool_rt, hseq(qls), hseq(qps),
                             to_seq_major(cks), to_seq_major(kps).transpose(0, 2, 1), n_pg=n_pg)
            ol = ol.reshape(n_s, b_heads, n_t, lora_kv).transpose(1, 2, 0, 3).reshape(
                b_heads, n_t * n_s, lora_kv)
            os_ = _head_matmul(ol, w_uv[j].transpose(1, 0, 2).astype(BF16))
            w_o = w_o_b[j].astype(BF16)
            outs["ckv_p"].append(with_meta(ckm, ckp))
            outs["kpe_p"].append(with_meta(kpm, kpp))
            outs["ckv_s"].append(to_seq_major(cks))
            outs["kpe_s"].append(to_seq_major(kps))
        xm = _proj_res(om, xm, w_o, g1)
        xp = _proj_res(op, xp, w_o, g1)
        xs = _proj_res(os_, xs, w_o, g1)

        w_in, w_out = w_ffn_in[i].astype(BF16), w_ffn_out[i].astype(BF16)
        cw, cb = conv_w[i].astype(F32), conv_b[i].astype(F32)[None]
        ffn = functools.partial(_conv_ffn, g2=g2, g3=g3, w_in=w_in, conv_w=cw, conv_b=cb, w_out=w_out)
        xm, cm = ffn(xm, state=jnp.zeros((1, 2, ff), F32), n_seq=1, shift=1, tm_cap=512)
        xp, cp = ffn(xp, state=jnp.broadcast_to(cm, (n_b, 2, ff)), n_seq=n_b, shift=1, tm_cap=512)
        st = state_conv[i].astype(F32).transpose(1, 0, 2).reshape(1, 2 * n_s, ff)
        xs, cs = ffn(xs, state=st, n_seq=1, shift=n_s, tm_cap=2 * n_s)
        outs["conv_p"].append(cp)
        outs["conv_s"].append(cs.reshape(2, n_s, ff).transpose(1, 0, 2))

    y_prompt = xp.reshape(n_b, seq, d)
    y_sample = to_seq_major(xs)
    return (y_prompt, y_sample,
            jnp.stack(outs["k_a_p"]), jnp.stack(outs["v_a_p"]), jnp.stack(outs["k_a_s"]),
            jnp.stack(outs["v_a_s"]), jnp.stack(outs["ckv_p"]), jnp.stack(outs["kpe_p"]),
            jnp.stack(outs["ckv_s"]), jnp.stack(outs["kpe_s"]), jnp.stack(outs["conv_p"]),
            jnp.stack(outs["conv_s"]))
```
